```python
import jax
import jax.numpy as jnp
from jax import lax
import numpy as np

D_MODEL = 2048
BATCH = 8
SEQ = 2048
DEPTH = 4

GRID_W = 64
CTX_LEN = 256
F32 = jnp.float32
EPS = 1e-6
ROPE_BASE = 10000.0
N_MOD = 6
N_EVEN = (DEPTH + 1) // 2
N_ODD = DEPTH // 2

FOURIER_GROUPS = 4
FOURIER_GROUP_DIM = D_MODEL // 16
FOURIER_DIM = FOURIER_GROUPS * FOURIER_GROUP_DIM
ATTN_HEAD_DIM = 128
ATTN_HEADS = (D_MODEL - FOURIER_DIM) // ATTN_HEAD_DIM
ATTN_KV_HEADS = 4
ATTN_GROUP = ATTN_HEADS // ATTN_KV_HEADS
ATTN_Q_BLOCK = 128
EVEN_IN_DIM = FOURIER_DIM + (ATTN_HEADS + 2 * ATTN_KV_HEADS) * ATTN_HEAD_DIM
EVEN_MIX_DIM = FOURIER_DIM + ATTN_HEADS * ATTN_HEAD_DIM

RET_HEADS = 8
RET_QK_DIM = D_MODEL // RET_HEADS
RET_V_DIM = 2 * RET_QK_DIM
RET_CHUNK = 128
ODD_IN_DIM = 2 * RET_HEADS * RET_QK_DIM + 2 * RET_HEADS * RET_V_DIM
ODD_MIX_DIM = RET_HEADS * RET_V_DIM

FFN_DIM = -(-8 * D_MODEL // 768) * 256

kernel_name = 'hybrid_fourier_gqa_retention_dit'


def rms_norm(x, eps=EPS):
    xf = x.astype(F32)
    return (xf * lax.rsqrt(jnp.mean(xf * xf, axis=-1, keepdims=True) + eps)).astype(x.dtype)


def modulate(x, shift, scale):
    return rms_norm(x) * (1.0 + scale) + shift


def grid_positions(n_tokens):
    rows = n_tokens // GRID_W
    row = jnp.repeat(jnp.arange(rows, dtype=jnp.int32), GRID_W)
    col = jnp.tile(jnp.arange(GRID_W, dtype=jnp.int32), rows)
    return row, col


def rope_1d(x, pos):
    dp = x.shape[-1]
    inv = ROPE_BASE ** (-jnp.arange(0, dp, 2, dtype=F32) / dp)
    ang = pos.astype(F32)[:, None] * inv[None, :]
    cos = jnp.cos(ang)[None, :, None, :]
    sin = jnp.sin(ang)[None, :, None, :]
    x1, x2 = jnp.split(x.astype(F32), 2, axis=-1)
    return jnp.concatenate([x1 * cos - x2 * sin, x2 * cos + x1 * sin], axis=-1).astype(x.dtype)


def axial_rope(x, row, col):
    half = x.shape[-1] // 2
    return jnp.concatenate([rope_1d(x[..., :half], row), rope_1d(x[..., half:], col)], axis=-1)


def head_rms_norm(x, gain):
    return rms_norm(x) * gain


def fourier_mix(u):
    b, t, _ = u.shape
    uf = u.astype(F32).reshape(b, t, FOURIER_GROUPS, FOURIER_GROUP_DIM)
    y = jnp.fft.fftn(uf, axes=(1, 3), norm='ortho').real
    return y.reshape(b, t, FOURIER_DIM).astype(u.dtype)


def softmax_attend(q, k, v):
    s = jnp.einsum('bqkgd,bskd->bkgqs', q.astype(F32), k.astype(F32)) * (ATTN_HEAD_DIM ** -0.5)
    p = jax.nn.softmax(s, axis=-1)
    return jnp.einsum('bkgqs,bskd->bqkgd', p, v.astype(F32)).astype(q.dtype)


def blocked_attend(q, k, v):
    b, t = q.shape[:2]
    nb = t // ATTN_Q_BLOCK
    qb = jnp.moveaxis(q.reshape(b, nb, ATTN_Q_BLOCK, *q.shape[2:]), 1, 0)
    ob = lax.map(lambda blk: softmax_attend(blk, k, v), qb)
    return jnp.moveaxis(ob, 0, 1).reshape(q.shape)


def split_even(p):
    b, t, _ = p.shape
    hq = ATTN_HEADS * ATTN_HEAD_DIM
    hkv = ATTN_KV_HEADS * ATTN_HEAD_DIM
    f, q, k, v = jnp.split(p, [FOURIER_DIM, FOURIER_DIM + hq, FOURIER_DIM + hq + hkv], axis=-1)
    return (f, q.reshape(b, t, ATTN_HEADS, ATTN_HEAD_DIM),
            k.reshape(b, t, ATTN_KV_HEADS, ATTN_HEAD_DIM),
            v.reshape(b, t, ATTN_KV_HEADS, ATTN_HEAD_DIM))


def group_heads(q):
    b, t = q.shape[:2]
    return q.reshape(b, t, ATTN_KV_HEADS, ATTN_GROUP, ATTN_HEAD_DIM)


def fourier_gqa_mixer(u_ctx, u_lat, w_in, w_out, q_gain, k_gain, need_ctx):
    b, n_lat, _ = u_lat.shape
    row, col = grid_positions(n_lat)
    f_c, q_c, k_c, v_c = split_even(u_ctx @ w_in)
    f_l, q_l, k_l, v_l = split_even(u_lat @ w_in)
    k_c = head_rms_norm(k_c, k_gain)
    k_l = axial_rope(head_rms_norm(k_l, k_gain), row, col)
    q_l = axial_rope(head_rms_norm(q_l, q_gain), row, col)
    k_all = jnp.concatenate([k_l, k_c], axis=1)
    v_all = jnp.concatenate([v_l, v_c], axis=1)
    a_l = blocked_attend(group_heads(q_l), k_all, v_all).reshape(b, n_lat, -1)
    o_l = jnp.concatenate([fourier_mix(f_l), a_l], axis=-1) @ w_out
    o_c = None
    if need_ctx:
        q_c = head_rms_norm(q_c, q_gain)
        a_c = softmax_attend(group_heads(q_c), k_c, v_c).reshape(b, u_ctx.shape[1], -1)
        o_c = jnp.concatenate([fourier_mix(f_c), a_c], axis=-1) @ w_out
    return o_c, o_l


def retention_scan(q, k, v, log_gamma, init_state, strict):
    b, h, t, _ = q.shape
    dv = v.shape[-1]
    n = t // RET_CHUNK
    idx = jnp.arange(RET_CHUNK, dtype=F32)
    diff = idx[:, None] - idx[None, :]
    lg = log_gamma.astype(F32)[:, None, None]
    mask = diff > 0 if strict else diff >= 0
    inner_decay = jnp.where(mask, jnp.exp(jnp.maximum(diff, 0.0) * lg), 0.0)
    q_decay = jnp.exp((idx + 1.0) * lg[:, :, 0])[..., None]
    k_decay = jnp.exp((RET_CHUNK - 1.0 - idx) * lg[:, :, 0])[..., None]
    chunk_decay = jnp.exp(RET_CHUNK * lg)

    def to_chunks(a):
        return jnp.moveaxis(a.reshape(b, h, n, RET_CHUNK, a.shape[-1]), 2, 0)

    def step(state, inp):
        qc, kc, vc = inp
        scores = jnp.einsum('bhid,bhjd->bhij', qc, kc) * inner_decay
        out = (jnp.einsum('bhij,bhje->bhie', scores, vc)
               + jnp.einsum('bhid,bhde->bhie', qc * q_decay, state))
        state = state * chunk_decay + jnp.einsum('bhjd,bhje->bhde', kc * k_decay, vc)
        return state, out

    final, out = lax.scan(step, init_state, (to_chunks(q), to_chunks(k), to_chunks(v)))
    return jnp.moveaxis(out, 0, 2).reshape(b, h, t, dv), final


def bidir_retention(q, k, v, lg_fwd, lg_bwd, init_fwd, init_bwd):
    o_f, s_f = retention_scan(q, k, v, lg_fwd, init_fwd, False)
    flip = lambda a: jnp.flip(a, axis=2)
    o_b, s_b = retention_scan(flip(q), flip(k), flip(v), lg_bwd, init_bwd, True)
    return o_f + flip(o_b), s_f, s_b


def split_odd(p):
    b, t, _ = p.shape
    dqk = RET_HEADS * RET_QK_DIM
    dvv = RET_HEADS * RET_V_DIM
    q, k, v, g = jnp.split(p, [dqk, 2 * dqk, 2 * dqk + dvv], axis=-1)
    return (q.reshape(b, t, RET_HEADS, RET_QK_DIM), k.reshape(b, t, RET_HEADS, RET_QK_DIM),
            v.reshape(b, t, RET_HEADS, RET_V_DIM), g)


def retention_mixer(u_ctx, u_lat, w_in, w_out, lg_fwd, lg_bwd, need_ctx):
    b, n_lat, _ = u_lat.shape
    row, col = grid_positions(n_lat)
    q_c, k_c, v_c, g_c = split_odd(u_ctx @ w_in)
    q_l, k_l, v_l, g_l = split_odd(u_lat @ w_in)
    q_l = axial_rope(q_l, row, col)
    k_l = axial_rope(k_l, row, col)
    bhtd = lambda a: jnp.swapaxes(a, 1, 2).astype(F32)
    k_scale = RET_QK_DIM ** -0.5
    zero = jnp.zeros((b, RET_HEADS, RET_QK_DIM, RET_V_DIM), F32)
    o_c, s_f, s_b = bidir_retention(bhtd(q_c), bhtd(k_c) * k_scale, bhtd(v_c), lg_fwd, lg_bwd, zero, zero)
    o_l, _, _ = bidir_retention(bhtd(q_l), bhtd(k_l) * k_scale, bhtd(v_l), lg_fwd, lg_bwd, s_f, s_b)

    def finish(o, g):
        t = o.shape[2]
        o = rms_norm(jnp.swapaxes(o, 1, 2)).reshape(b, t, ODD_MIX_DIM).astype(g.dtype)
        return (jax.nn.silu(g) * o) @ w_out

    out_l = finish(o_l, g_l)
    out_c = finish(o_c, g_c) if need_ctx else None
    return out_c, out_l


def swiglu(u, w_in, w_out):
    gate, up = jnp.split(u @ w_in, 2, axis=-1)
    return (jax.nn.silu(gate) * up) @ w_out


def setup_inputs(seed: int = 0) -> dict:
    key = jax.random.key(seed)
    ks = jax.random.split(key, 17)

    def dense(k, shape, fan_in):
        return jax.random.normal(k, shape, F32) * (fan_in ** -0.5)

    ret_base = jnp.log1p(-jnp.exp2(-5.0 - jnp.arange(RET_HEADS, dtype=F32)))
    return {
        'x': jax.random.normal(ks[0], (BATCH, SEQ, D_MODEL), F32),
        'c': jax.random.normal(ks[1], (BATCH, D_MODEL), F32),
        'ctx': jax.random.normal(ks[2], (BATCH, CTX_LEN, D_MODEL), F32),
        'c_ctx': jax.random.normal(ks[3], (D_MODEL,), F32),
        'w_mod': dense(ks[4], (DEPTH, D_MODEL, N_MOD * D_MODEL), D_MODEL),
        'b_mod': 0.01 * jax.random.normal(ks[5], (DEPTH, N_MOD * D_MODEL), F32),
        'w_in_even': dense(ks[6], (N_EVEN, D_MODEL, EVEN_IN_DIM), D_MODEL),
        'w_out_even': dense(ks[7], (N_EVEN, EVEN_MIX_DIM, D_MODEL), EVEN_MIX_DIM),
        'q_gain_even': 1.0 + 0.02 * jax.random.normal(ks[8], (N_EVEN, ATTN_HEAD_DIM), F32),
        'k_gain_even': 1.0 + 0.02 * jax.random.normal(ks[9], (N_EVEN, ATTN_HEAD_DIM), F32),
        'w_in_odd': dense(ks[10], (N_ODD, D_MODEL, ODD_IN_DIM), D_MODEL),
        'w_out_odd': dense(ks[11], (N_ODD, ODD_MIX_DIM, D_MODEL), ODD_MIX_DIM),
        'log_decay_fwd': ret_base[None] * (1.0 + 0.05 * jax.random.normal(ks[12], (N_ODD, RET_HEADS), F32)),
        'log_decay_bwd': ret_base[None] * (1.0 + 0.05 * jax.random.normal(ks[13], (N_ODD, RET_HEADS), F32)),
        'w_ffn_in': dense(ks[14], (DEPTH, D_MODEL, 2 * FFN_DIM), D_MODEL),
        'w_ffn_out': dense(ks[15], (DEPTH, FFN_DIM, D_MODEL), FFN_DIM),
    }


def reference(x, c, ctx, c_ctx, w_mod, b_mod, w_in_even, w_out_even, q_gain_even, k_gain_even,
              w_in_odd, w_out_odd, log_decay_fwd, log_decay_bwd, w_ffn_in, w_ffn_out):
    h_lat, h_ctx = x, ctx
    cond_lat = jax.nn.silu(c)
    cond_ctx = jax.nn.silu(c_ctx)[None]
    for i in range(DEPTH):
        need_ctx = i < DEPTH - 1
        mod_l = (cond_lat @ w_mod[i] + b_mod[i])[:, None, :]
        mod_c = (cond_ctx @ w_mod[i] + b_mod[i])[:, None, :]
        sh1, sc1, g1, sh2, sc2, g2 = jnp.split(mod_l, N_MOD, axis=-1)
        csh1, csc1, cg1, csh2, csc2, cg2 = jnp.split(mod_c, N_MOD, axis=-1)
        u_l = modulate(h_lat, sh1, sc1)
        u_c = modulate(h_ctx, csh1, csc1)
        j = i // 2
        if i % 2 == 0:
            o_c, o_l = fourier_gqa_mixer(u_c, u_l, w_in_even[j], w_out_even[j],
                                         q_gain_even[j], k_gain_even[j], need_ctx)
        else:
            o_c, o_l = retention_mixer(u_c, u_l, w_in_odd[j], w_out_odd[j],
                                       log_decay_fwd[j], log_decay_bwd[j], need_ctx)
        h_lat = h_lat + g1 * o_l
        h_lat = h_lat + g2 * swiglu(modulate(h_lat, sh2, sc2), w_ffn_in[i], w_ffn_out[i])
        if need_ctx:
            h_ctx = h_ctx + cg1 * o_c
            h_ctx = h_ctx + cg2 * swiglu(modulate(h_ctx, csh2, csc2), w_ffn_in[i], w_ffn_out[i])
    return h_lat
```

```python
import functools
import math

import jax
import jax.numpy as jnp
from jax import lax
from jax.experimental import pallas as pl
from jax.experimental.pallas import tpu as pltpu

F32 = jnp.float32
BF16 = jnp.bfloat16

EPS = 1e-6
ROPE_BASE = 10000.0
GRID_W = 64
N_MOD = 6
FOURIER_GROUPS = 4
ATTN_HEAD_DIM = 128
ATTN_KV_HEADS = 4

V7X_LANES = 128
V7X_VMEM_BYTES = 64 * 1024 * 1024
VMEM_LIMIT_BYTES = V7X_VMEM_BYTES - 8 * 1024 * 1024


def _tile(n, target, align=8):
    best = None
    for t in range(align, min(n, target) + 1, align):
        if n % t == 0:
            best = t
    return n if best is None else best


def _params(*semantics):
    return pltpu.CompilerParams(dimension_semantics=semantics, vmem_limit_bytes=VMEM_LIMIT_BYTES)


def _silu(x):
    return x * jax.nn.sigmoid(x)


def _norm_mod(x, shift, scale):
    ms = jnp.mean(x * x, axis=-1, keepdims=True)
    return x * lax.rsqrt(ms + EPS) * (1.0 + scale) + shift


def _mod_spec(mod, rows_per_mod, tm, extra_axes=0):
    tiles = rows_per_mod // tm
    d = mod.shape[-1]
    if extra_axes == 0:
        return pl.BlockSpec((None, 1, d), lambda i: (i // tiles, 0, 0))
    return pl.BlockSpec((None, 1, d), lambda i, j: (i // tiles, 0, 0))


def _mod_kernel(c_ref, w_ref, b_ref, o_ref):
    cond = _silu(c_ref[...]).astype(BF16)
    o_ref[...] = jnp.dot(cond, w_ref[...].astype(BF16), preferred_element_type=F32) + b_ref[...]


def _modulation(c_all, w_mod, b_mod):
    depth, d, n = w_mod.shape
    r = c_all.shape[0]
    tn = _tile(n, 1024, V7X_LANES)
    return pl.pallas_call(
        _mod_kernel,
        grid=(depth, n // tn),
        in_specs=[
            pl.BlockSpec((r, d), lambda i, j: (0, 0)),
            pl.BlockSpec((None, d, tn), lambda i, j: (i, 0, j)),
            pl.BlockSpec((None, 1, tn), lambda i, j: (i, 0, j)),
        ],
        out_specs=pl.BlockSpec((None, r, tn), lambda i, j: (i, 0, j)),
        out_shape=jax.ShapeDtypeStruct((depth, r, n), F32),
        compiler_params=_params("parallel", "parallel"),
        name="modulation",
    )(c_all, w_mod, b_mod.reshape(depth, 1, n))


def _rope_angles(n_tokens, dp):
    t = jnp.arange(n_tokens, dtype=jnp.int32)
    inv = ROPE_BASE ** (-jnp.arange(0, dp, 2, dtype=F32) / dp)
    row = (t // GRID_W).astype(F32)[:, None] * inv[None, :]
    col = (t % GRID_W).astype(F32)[:, None] * inv[None, :]
    return row, col


def _rope_tables(n_tokens, head_dim, identity):
    half = head_dim // 2
    if identity:
        return jnp.ones((n_tokens, head_dim), F32), jnp.zeros((n_tokens, head_dim), F32)
    row, col = _rope_angles(n_tokens, half)
    cos = jnp.concatenate([jnp.cos(row), jnp.cos(row), jnp.cos(col), jnp.cos(col)], axis=-1)
    sin = jnp.concatenate([-jnp.sin(row), jnp.sin(row), -jnp.sin(col), jnp.sin(col)], axis=-1)
    return cos, sin


def _rope_apply(y, cos, sin, quarter):
    if 2 * quarter == V7X_LANES:
        return y * cos + pltpu.roll(y, quarter, 1) * sin
    lane = lax.broadcasted_iota(jnp.int32, y.shape, 1)
    first = (lane % (2 * quarter)) < quarter
    partner = jnp.where(first, pltpu.roll(y, V7X_LANES - quarter, 1), pltpu.roll(y, quarter, 1))
    return y * cos + partner * sin


def _proj_even_kernel(x_ref, sh_ref, sc_ref, w_ref, qg_ref, kg_ref, cos_ref, sin_ref,
                      f_ref, q_ref, k_ref, v_ref, *, fdim, qdim, kvdim, hd):
    u = _norm_mod(x_ref[...], sh_ref[...], sc_ref[...]).astype(BF16)
    cos = cos_ref[...]
    sin = sin_ref[...]

    def heads(lo, width, gain_ref, out_ref):
        seg = _tile(width, 512, hd)
        gain = gain_ref[...]
        for s in range(width // seg):
            y = jnp.dot(u, w_ref[:, lo + s * seg:lo + (s + 1) * seg], preferred_element_type=F32)
            for h in range(seg // hd):
                yh = y[:, h * hd:(h + 1) * hd]
                ms = jnp.mean(yh * yh, axis=-1, keepdims=True)
                yh = yh * lax.rsqrt(ms + EPS) * gain
                yh = _rope_apply(yh, cos, sin, hd // 4)
                out_ref[:, s * seg + h * hd:s * seg + (h + 1) * hd] = yh.astype(BF16)

    f_ref[...] = jnp.dot(u, w_ref[:, :fdim], preferred_element_type=F32).astype(BF16)
    heads(fdim, qdim, qg_ref, q_ref)
    heads(fdim + qdim, kvdim, kg_ref, k_ref)
    v_ref[...] = jnp.dot(u, w_ref[:, fdim + qdim + kvdim:], preferred_element_type=F32).astype(BF16)


def _proj_even(h, shift, scale, rows_per_mod, w, q_gain, k_gain, cos, sin, dims):
    n, d = h.shape
    fdim, qdim, kvdim = dims
    hd = ATTN_HEAD_DIM
    pos_rows = cos.shape[0]
    tm = _tile(math.gcd(rows_per_mod, pos_rows), 512)
    pos_tiles = pos_rows // tm
    kern = functools.partial(_proj_even_kernel, fdim=fdim, qdim=qdim, kvdim=kvdim, hd=hd)
    row = lambda width: pl.BlockSpec((tm, width), lambda i: (i, 0))
    const = lambda a: pl.BlockSpec(a.shape, lambda i: (0,) * a.ndim)
    return pl.pallas_call(
        kern,
        grid=(n // tm,),
        in_specs=[
            row(d), _mod_spec(shift, rows_per_mod, tm), _mod_spec(scale, rows_per_mod, tm),
            const(w), const(q_gain), const(k_gain),
            pl.BlockSpec((tm, hd), lambda i: (i % pos_tiles, 0)),
            pl.BlockSpec((tm, hd), lambda i: (i % pos_tiles, 0)),
        ],
        out_specs=[row(fdim), row(qdim), row(kvdim), row(kvdim)],
        out_shape=[jax.ShapeDtypeStruct((n, fdim), BF16), jax.ShapeDtypeStruct((n, qdim), BF16),
                   jax.ShapeDtypeStruct((n, kvdim), BF16), jax.ShapeDtypeStruct((n, kvdim), BF16)],
        compiler_params=_params("parallel"),
        name="proj_even",
    )(h, shift, scale, w, q_gain, k_gain, cos, sin)


def _attn_kernel(q_ref, *refs, n_seg, group, hd, scale):
    k_refs, v_refs, o_ref = refs[:n_seg], refs[n_seg:2 * n_seg], refs[2 * n_seg]
    q = q_ref[...]
    tq = q.shape[0]
    qs = jnp.concatenate([q[:, g * hd:(g + 1) * hd] for g in range(group)], axis=0)
    nt = (((1,), (1,)), ((), ()))
    scores = [lax.dot_general(qs, k[...], nt, preferred_element_type=F32) for k in k_refs]
    m = functools.reduce(jnp.maximum, [jnp.max(s, axis=-1, keepdims=True) for s in scores])
    ps = [jnp.exp((s - m) * scale) for s in scores]
    denom = functools.reduce(jnp.add, [jnp.sum(p, axis=-1, keepdims=True) for p in ps])
    o = functools.reduce(jnp.add, [jnp.dot(p.astype(BF16), v[...], preferred_element_type=F32)
                                   for p, v in zip(ps, v_refs)])
    o = o / denom
    for g in range(group):
        o_ref[:, g * hd:(g + 1) * hd] = o[g * tq:(g + 1) * tq].astype(BF16)


def _attention(q, segments, batch):
    hd, kvh = ATTN_HEAD_DIM, ATTN_KV_HEADS
    n, qdim = q.shape
    group = qdim // (kvh * hd)
    t_q = n // batch
    tq = _tile(t_q, 256)
    nq = t_q // tq
    kern = functools.partial(_attn_kernel, n_seg=len(segments), group=group, hd=hd, scale=hd ** -0.5)
    kv_spec = lambda a: pl.BlockSpec((a.shape[0] // batch, hd), lambda b, kh, t: (b, kh))
    q_spec = pl.BlockSpec((tq, group * hd), lambda b, kh, t: (b * nq + t, kh))
    return pl.pallas_call(
        kern,
        grid=(batch, kvh, nq),
        in_specs=[q_spec] + [kv_spec(k) for k, _ in segments] + [kv_spec(v) for _, v in segments],
        out_specs=q_spec,
        out_shape=jax.ShapeDtypeStruct((n, qdim), BF16),
        compiler_params=_params("parallel", "parallel", "parallel"),
        name="attention",
    )(q, *[k for k, _ in segments], *[v for _, v in segments])


def _fourier_kernel(u_ref, cm_ref, cs_ref, o_ref, z_scr, *, groups, gd, scale):
    t = u_ref.shape[0]

    @pl.when(pl.program_id(1) == 0)
    def _():
        for g in range(groups):
            cols = slice(g * gd, (g + 1) * gd)
            z = jnp.dot(u_ref[:, cols], cm_ref[...], preferred_element_type=F32)
            z_scr[0:t, cols] = z[:, :gd].astype(BF16)
            z_scr[t:2 * t, cols] = z[:, gd:].astype(BF16)

    y = jnp.dot(cs_ref[...], z_scr[...], preferred_element_type=F32)
    o_ref[...] = (y * scale).astype(BF16)


def _dft_tables(t, gd):
    def angles(n):
        i = jnp.arange(n, dtype=jnp.int32)
        return ((i[:, None] * i[None, :]) % n).astype(F32) * (2.0 * math.pi / n)
    a_c, a_t = angles(gd), angles(t)
    cm = jnp.concatenate([jnp.cos(a_c), -jnp.sin(a_c)], axis=1).astype(BF16)
    cs = jnp.concatenate([jnp.cos(a_t), jnp.sin(a_t)], axis=1).astype(BF16)
    return cm, cs


def _fourier(u, batch, tables):
    n, fdim = u.shape
    t = n // batch
    gd = fdim // FOURIER_GROUPS
    cm, cs = tables
    tr = _tile(t, 512)
    kern = functools.partial(_fourier_kernel, groups=FOURIER_GROUPS, gd=gd, scale=(t * gd) ** -0.5)
    return pl.pallas_call(
        kern,
        grid=(batch, t // tr),
        in_specs=[
            pl.BlockSpec((t, fdim), lambda b, r: (b, 0)),
            pl.BlockSpec(cm.shape, lambda b, r: (0, 0)),
            pl.BlockSpec((tr, 2 * t), lambda b, r: (r, 0)),
        ],
        out_specs=pl.BlockSpec((tr, fdim), lambda b, r: (b * (t // tr) + r, 0)),
        out_shape=jax.ShapeDtypeStruct((n, fdim), BF16),
        scratch_shapes=[pltpu.VMEM((2 * t, fdim), BF16)],
        compiler_params=_params("parallel", "arbitrary"),
        name="fourier",
    )(u, cm, cs)


def _proj_odd_kernel(x_ref, sh_ref, sc_ref, w_ref, cos_ref, sin_ref, o_ref, u_scr,
                     *, q_tiles, qk_tiles, hd, k_scale):
    j = pl.program_id(1)

    @pl.when(j == 0)
    def _():
        u_scr[...] = _norm_mod(x_ref[...], sh_ref[...], sc_ref[...]).astype(BF16)

    y = jnp.dot(u_scr[...], w_ref[...], preferred_element_type=F32)
    tn = y.shape[1]

    @pl.when(j < qk_tiles)
    def _():
        mult = jnp.where(j < q_tiles, 1.0, k_scale).astype(F32)
        for c in range(tn // V7X_LANES):
            lanes = slice(c * V7X_LANES, (c + 1) * V7X_LANES)
            tab = slice((c % (hd // V7X_LANES)) * V7X_LANES, (c % (hd // V7X_LANES) + 1) * V7X_LANES)
            r = _rope_apply(y[:, lanes], cos_ref[:, tab], sin_ref[:, tab], hd // 4)
            o_ref[:, lanes] = (r * mult).astype(BF16)

    @pl.when(j >= qk_tiles)
    def _():
        o_ref[...] = y.astype(BF16)


def _proj_odd(h, shift, scale, rows_per_mod, w, cos, sin, qk_dim):
    n, d = h.shape
    n_out = w.shape[1]
    hd = cos.shape[1]
    pos_rows = cos.shape[0]
    tm = _tile(math.gcd(rows_per_mod, pos_rows), 1024)
    tn = _tile(math.gcd(qk_dim, n_out), 1024, hd)
    pos_tiles = pos_rows // tm
    kern = functools.partial(_proj_odd_kernel, q_tiles=qk_dim // tn, qk_tiles=2 * qk_dim // tn,
                             hd=hd, k_scale=hd ** -0.5)
    return pl.pallas_call(
        kern,
        grid=(n // tm, n_out // tn),
        in_specs=[
            pl.BlockSpec((tm, d), lambda i, j: (i, 0)),
            _mod_spec(shift, rows_per_mod, tm, 1), _mod_spec(scale, rows_per_mod, tm, 1),
            pl.BlockSpec((d, tn), lambda i, j: (0, j)),
            pl.BlockSpec((tm, hd), lambda i, j: (i % pos_tiles, 0)),
            pl.BlockSpec((tm, hd), lambda i, j: (i % pos_tiles, 0)),
        ],
        out_specs=pl.BlockSpec((tm, tn), lambda i, j: (i, j)),
        out_shape=jax.ShapeDtypeStruct((n, n_out), BF16),
        scratch_shapes=[pltpu.VMEM((tm, d), BF16)],
        compiler_params=_params("parallel", "arbitrary"),
        name="proj_odd",
    )(h, shift, scale, w, cos, sin)


def _retention_kernel(lgf_ref, lgb_ref, qc_ref, kc_ref, vc_ref, gc_ref, ql_ref, kl_ref, vl_ref, gl_ref,
                      oc_ref, ol_ref, sf_scr, sb_scr, d_scr, accc_scr, accl_scr, *, chunk):
    c = chunk
    lgf = lgf_ref[...][:, :1]
    lgb = lgb_ref[...][:, :1]
    diff = (lax.broadcasted_iota(jnp.int32, (c, c), 0)
            - lax.broadcasted_iota(jnp.int32, (c, c), 1)).astype(F32)
    d_scr[...] = jnp.where(diff >= 0.0, jnp.exp(jnp.maximum(diff, 0.0) * lgf),
                           jnp.exp(jnp.maximum(-diff, 0.0) * lgb))
    idx = lax.broadcasted_iota(jnp.int32, (c, 1), 0).astype(F32)
    q_dec_f, k_dec_f = jnp.exp((idx + 1.0) * lgf), jnp.exp((c - 1.0 - idx) * lgf)
    q_dec_b, k_dec_b = jnp.exp((c - idx) * lgb), jnp.exp(idx * lgb)
    chunk_dec_f, chunk_dec_b = jnp.exp(c * lgf), jnp.exp(c * lgb)
    nt = (((1,), (1,)), ((), ()))
    tn = (((0,), (0,)), ((), ()))

    sf_scr[...] = jnp.zeros_like(sf_scr)
    sb_scr[...] = jnp.zeros_like(sb_scr)

    def forward(q_ref, k_ref, v_ref, acc_ref, i):
        rows = pl.ds(pl.multiple_of(i * c, c), c)
        q, k, v = q_ref[rows, :], k_ref[rows, :], v_ref[rows, :]
        s = lax.dot_general(q, k, nt, preferred_element_type=F32) * d_scr[...]
        o = jnp.dot(s.astype(BF16), v, preferred_element_type=F32)
        o = o + q_dec_f * jnp.dot(q, sf_scr[...].astype(BF16), preferred_element_type=F32)
        kd = (k.astype(F32) * k_dec_f).astype(BF16)
        sf_scr[...] = sf_scr[...] * chunk_dec_f + lax.dot_general(kd, v, tn, preferred_element_type=F32)
        acc_ref[rows, :] = o

    def backward(q_ref, k_ref, v_ref, g_ref, acc_ref, o_ref, i):
        rows = pl.ds(pl.multiple_of(i * c, c), c)
        q, k, v = q_ref[rows, :], k_ref[rows, :], v_ref[rows, :]
        o = acc_ref[rows, :] + q_dec_b * jnp.dot(q, sb_scr[...].astype(BF16), preferred_element_type=F32)
        kd = (k.astype(F32) * k_dec_b).astype(BF16)
        sb_scr[...] = sb_scr[...] * chunk_dec_b + lax.dot_general(kd, v, tn, preferred_element_type=F32)
        o = o * lax.rsqrt(jnp.mean(o * o, axis=-1, keepdims=True) + EPS)
        o_ref[rows, :] = (_silu(g_ref[rows, :].astype(F32)) * o).astype(BF16)

    def sweep(fn, n_chunks, reverse):
        def body(t, carry):
            fn(n_chunks - 1 - t if reverse else t)
            return carry
        lax.fori_loop(0, n_chunks, body, 0)

    n_c, n_l = qc_ref.shape[0] // c, ql_ref.shape[0] // c
    sweep(functools.partial(forward, qc_ref, kc_ref, vc_ref, accc_scr), n_c, False)
    sweep(functools.partial(forward, ql_ref, kl_ref, vl_ref, accl_scr), n_l, False)
    sweep(functools.partial(backward, qc_ref, kc_ref, vc_ref, gc_ref, accc_scr, oc_ref), n_c, True)
    sweep(functools.partial(backward, ql_ref, kl_ref, vl_ref, gl_ref, accl_scr, ol_ref), n_l, True)


def _retention(p_ctx, p_lat, lg_fwd, lg_bwd, batch):
    heads = lg_fwd.shape[0]
    n_out = p_lat.shape[1]
    dk = n_out // (6 * heads)
    dv = 2 * dk
    t_c, t_l = p_ctx.shape[0] // batch, p_lat.shape[0] // batch
    c = _tile(math.gcd(t_c, t_l), 256)
    lanes = lambda lg: jnp.broadcast_to(lg.astype(F32)[:, None, None], (heads, 1, V7X_LANES))
    lg_spec = pl.BlockSpec((None, 1, V7X_LANES), lambda b, h: (h, 0, 0))

    def specs(t):
        return [pl.BlockSpec((t, dk), lambda b, h: (b, h)),
                pl.BlockSpec((t, dk), lambda b, h: (b, heads + h)),
                pl.BlockSpec((t, dv), lambda b, h: (b, heads + h)),
                pl.BlockSpec((t, dv), lambda b, h: (b, 2 * heads + h))]

    out_spec = lambda t: pl.BlockSpec((t, dv), lambda b, h: (b, h))
    return pl.pallas_call(
        functools.partial(_retention_kernel, chunk=c),
        grid=(batch, heads),
        in_specs=[lg_spec, lg_spec] + specs(t_c) + specs(t_l),
        out_specs=[out_spec(t_c), out_spec(t_l)],
        out_shape=[jax.ShapeDtypeStruct((p_ctx.shape[0], heads * dv), BF16),
                   jax.ShapeDtypeStruct((p_lat.shape[0], heads * dv), BF16)],
        scratch_shapes=[pltpu.VMEM((dk, dv), F32), pltpu.VMEM((dk, dv), F32), pltpu.VMEM((c, c), F32),
                        pltpu.VMEM((t_c, dv), F32), pltpu.VMEM((t_l, dv), F32)],
        compiler_params=_params("parallel", "parallel"),
        name="retention",
    )(lanes(lg_fwd), lanes(lg_bwd), p_ctx, p_ctx, p_ctx, p_ctx, p_lat, p_lat, p_lat, p_lat)


def _outproj_kernel(*refs, n_parts):
    a_refs, w_refs = refs[:n_parts], refs[n_parts:2 * n_parts]
    res_ref, gate_ref, o_ref = refs[2 * n_parts:]
    acc = functools.reduce(jnp.add, [jnp.dot(a[...], w[...], preferred_element_type=F32)
                                     for a, w in zip(a_refs, w_refs)])
    o_ref[...] = res_ref[...] + gate_ref[...] * acc


def _outproj(parts, weights, res, gate, rows_per_mod):
    n, d = res.shape
    tm = _tile(rows_per_mod, 1024)
    tn = _tile(d, 512, V7X_LANES)
    tiles = rows_per_mod // tm
    return pl.pallas_call(
        functools.partial(_outproj_kernel, n_parts=len(parts)),
        grid=(n // tm, d // tn),
        in_specs=[pl.BlockSpec((tm, a.shape[1]), lambda i, j: (i, 0)) for a in parts]
        + [pl.BlockSpec((w.shape[0], tn), lambda i, j: (0, j)) for w in weights]
        + [pl.BlockSpec((tm, tn), lambda i, j: (i, j)),
           pl.BlockSpec((None, 1, tn), lambda i, j: (i // tiles, 0, j))],
        out_specs=pl.BlockSpec((tm, tn), lambda i, j: (i, j)),
        out_shape=jax.ShapeDtypeStruct((n, d), F32),
        compiler_params=_params("parallel", "parallel"),
        name="outproj",
    )(*parts, *weights, res, gate)


def _ffn_kernel(x_ref, sh_ref, sc_ref, gate_ref, wg_ref, wu_ref, wo_ref, o_ref, u_scr, acc_scr):
    j = pl.program_id(1)

    @pl.when(j == 0)
    def _():
        u_scr[...] = _norm_mod(x_ref[...], sh_ref[...], sc_ref[...]).astype(BF16)
        acc_scr[...] = jnp.zeros_like(acc_scr)

    u = u_scr[...]
    g = jnp.dot(u, wg_ref[...], preferred_element_type=F32)
    up = jnp.dot(u, wu_ref[...], preferred_element_type=F32)
    act = (_silu(g) * up).astype(BF16)
    acc_scr[...] += jnp.dot(act, wo_ref[...], preferred_element_type=F32)

    @pl.when(j == pl.num_programs(1) - 1)
    def _():
        o_ref[...] = x_ref[...] + gate_ref[...] * acc_scr[...]


def _ffn(h, shift, scale, gate, rows_per_mod, w_in, w_out):
    n, d = h.shape
    f = w_out.shape[0]
    tm = _tile(rows_per_mod, 512)
    tf = _tile(f, 512, V7X_LANES)
    nf = f // tf
    return pl.pallas_call(
        _ffn_kernel,
        grid=(n // tm, nf),
        in_specs=[
            pl.BlockSpec((tm, d), lambda i, j: (i, 0)),
            _mod_spec(shift, rows_per_mod, tm, 1), _mod_spec(scale, rows_per_mod, tm, 1),
            _mod_spec(gate, rows_per_mod, tm, 1),
            pl.BlockSpec((d, tf), lambda i, j: (0, j)),
            pl.BlockSpec((d, tf), lambda i, j: (0, nf + j)),
            pl.BlockSpec((tf, d), lambda i, j: (j, 0)),
        ],
        out_specs=pl.BlockSpec((tm, d), lambda i, j: (i, 0)),
        out_shape=jax.ShapeDtypeStruct((n, d), F32),
        scratch_shapes=[pltpu.VMEM((tm, d), BF16), pltpu.VMEM((tm, d), F32)],
        compiler_params=_params("parallel", "arbitrary"),
        name="ffn",
    )(h, shift, scale, gate, w_in, w_in, w_out)


def _even_layer(h_ctx, h_lat, mod_ctx, mod_lat, w_in, w_out, q_gain, k_gain, batch, need_ctx, tables):
    n_ctx, d = h_ctx.shape
    n_lat = h_lat.shape[0]
    hd = ATTN_HEAD_DIM
    fdim = FOURIER_GROUPS * (d // 16)
    kvdim = ATTN_KV_HEADS * hd
    qdim = w_in.shape[1] - fdim - 2 * kvdim
    dims = (fdim, qdim, kvdim)
    w_in = w_in.astype(BF16)
    w_f, w_a = w_out[:fdim].astype(BF16), w_out[fdim:].astype(BF16)
    q_gain, k_gain = q_gain.reshape(1, hd), k_gain.reshape(1, hd)

    f_c, q_c, k_c, v_c = _proj_even(h_ctx, mod_ctx[0], mod_ctx[1], n_ctx, w_in, q_gain, k_gain,
                                    *tables["rope_even_ctx"], dims)
    f_l, q_l, k_l, v_l = _proj_even(h_lat, mod_lat[0], mod_lat[1], n_lat // batch, w_in, q_gain, k_gain,
                                    *tables["rope_even_lat"], dims)
    a_l = _attention(q_l, [(k_l, v_l), (k_c, v_c)], batch)
    m_l = _fourier(f_l, batch, tables["dft_lat"])
    h_lat = _outproj([m_l, a_l], [w_f, w_a], h_lat, mod_lat[2], n_lat // batch)
    if need_ctx:
        a_c = _attention(q_c, [(k_c, v_c)], batch)
        m_c = _fourier(f_c, batch, tables["dft_ctx"])
        h_ctx = _outproj([m_c, a_c], [w_f, w_a], h_ctx, mod_ctx[2], n_ctx)
    return h_ctx, h_lat


def _odd_layer(h_ctx, h_lat, mod_ctx, mod_lat, w_in, w_out, lg_fwd, lg_bwd, batch, need_ctx, tables):
    n_ctx = h_ctx.shape[0]
    n_lat = h_lat.shape[0]
    heads = lg_fwd.shape[0]
    qk_dim = w_in.shape[1] // 6
    w_in, w_out = w_in.astype(BF16), w_out.astype(BF16)
    p_c = _proj_odd(h_ctx, mod_ctx[0], mod_ctx[1], n_ctx, w_in, *tables["rope_odd_ctx"], qk_dim)
    p_l = _proj_odd(h_lat, mod_lat[0], mod_lat[1], n_lat // batch, w_in, *tables["rope_odd_lat"], qk_dim)
    m_c, m_l = _retention(p_c, p_l, lg_fwd, lg_bwd, batch)
    h_lat = _outproj([m_l], [w_out], h_lat, mod_lat[2], n_lat // batch)
    if need_ctx:
        h_ctx = _outproj([m_c], [w_out], h_ctx, mod_ctx[2], n_ctx)
    return h_ctx, h_lat


def kernel(x, c, ctx, c_ctx, w_mod, b_mod, w_in_even, w_out_even, q_gain_even, k_gain_even,
           w_in_odd, w_out_odd, log_decay_fwd, log_decay_bwd, w_ffn_in, w_ffn_out):
    batch, seq, d = x.shape
    ctx_len = ctx.shape[1]
    depth = w_mod.shape[0]
    n_lat, n_ctx = batch * seq, batch * ctx_len
    h_lat, h_ctx = x.reshape(n_lat, d), ctx.reshape(n_ctx, d)

    rows = -(-(batch + 1) // 8) * 8
    c_all = jnp.concatenate([c, c_ctx[None], jnp.zeros((rows - batch - 1, d), F32)], axis=0)
    mods = _modulation(c_all, w_mod, b_mod).reshape(depth, rows, N_MOD, d)

    ret_hd = w_in_odd.shape[2] // (6 * log_decay_fwd.shape[1])
    ctx_rows = _tile(n_ctx, 1024)
    tables = {
        "rope_even_lat": _rope_tables(seq, ATTN_HEAD_DIM, False),
        "rope_even_ctx": _rope_tables(ctx_rows, ATTN_HEAD_DIM, True),
        "rope_odd_lat": _rope_tables(seq, ret_hd, False),
        "rope_odd_ctx": _rope_tables(ctx_rows, ret_hd, True),
        "dft_lat": _dft_tables(seq, d // 16),
        "dft_ctx": _dft_tables(ctx_len, d // 16),
    }

    for i in range(depth):
        need_ctx = i < depth - 1
        mod_lat = [mods[i, :batch, k][:, None, :] for k in range(N_MOD)]
        mod_ctx = [mods[i, batch:batch + 1, k][:, None, :] for k in range(N_MOD)]
        j = i // 2
        if i % 2 == 0:
            h_ctx, h_lat = _even_layer(h_ctx, h_lat, mod_ctx, mod_lat, w_in_even[j], w_out_even[j],
                                       q_gain_even[j], k_gain_even[j], batch, need_ctx, tables)
        else:
            h_ctx, h_lat = _odd_layer(h_ctx, h_lat, mod_ctx, mod_lat, w_in_odd[j], w_out_odd[j],
                                      log_decay_fwd[j], log_decay_bwd[j], batch, need_ctx, tables)
        w_in, w_out = w_ffn_in[i].astype(BF16), w_ffn_out[i].astype(BF16)
        h_lat = _ffn(h_lat, mod_lat[3], mod_lat[4], mod_lat[5], n_lat // batch, w_in, w_out)
        if need_ctx:
            h_ctx = _ffn(h_ctx, mod_ctx[3], mod_ctx[4], mod_ctx[5], n_ctx, w_in, w_out)
    return h_lat.reshape(batch, seq, d)
```

```python
import functools
import math

import jax
import jax.numpy as jnp
from jax import lax
from jax.experimental import pallas as pl
from jax.experimental.pallas import tpu as pltpu

F32 = jnp.float32
BF16 = jnp.bfloat16

EPS = 1e-6
ROPE_BASE = 10000.0
GRID_W = 64
N_MOD = 6
FOURIER_GROUPS = 4
ATTN_HEAD_DIM = 128
ATTN_KV_HEADS = 4

V7X_LANES = 128
V7X_VMEM_BYTES = 64 * 1024 * 1024
VMEM_LIMIT_BYTES = V7X_VMEM_BYTES - 8 * 1024 * 1024


def _tile(n, target, align=8):
    best = None
    for t in range(align, min(n, target) + 1, align):
        if n % t == 0:
            best = t
    return n if best is None else best


def _params(*semantics):
    return pltpu.CompilerParams(dimension_semantics=semantics, vmem_limit_bytes=VMEM_LIMIT_BYTES)


def _silu(x):
    return x * jax.nn.sigmoid(x)


def _norm_mod(x, shift, scale):
    ms = jnp.mean(x * x, axis=-1, keepdims=True)
    return x * lax.rsqrt(ms + EPS) * (1.0 + scale) + shift


def _mod_spec(mod, rows_per_mod, tm, extra_axes=0):
    tiles = rows_per_mod // tm
    d = mod.shape[-1]
    if extra_axes == 0:
        return pl.BlockSpec((None, 1, d), lambda i: (i // tiles, 0, 0))
    return pl.BlockSpec((None, 1, d), lambda i, j: (i // tiles, 0, 0))


def _mod_kernel(c_ref, w_ref, b_ref, o_ref):
    cond = _silu(c_ref[...]).astype(BF16)
    o_ref[...] = jnp.dot(cond, w_ref[...].astype(BF16), preferred_element_type=F32) + b_ref[...]


def _modulation(c_all, w_mod, b_mod):
    depth, d, n = w_mod.shape
    r = c_all.shape[0]
    tn = _tile(n, 1024, V7X_LANES)
    return pl.pallas_call(
        _mod_kernel,
        grid=(depth, n // tn),
        in_specs=[
            pl.BlockSpec((r, d), lambda i, j: (0, 0)),
            pl.BlockSpec((None, d, tn), lambda i, j: (i, 0, j)),
            pl.BlockSpec((None, 1, tn), lambda i, j: (i, 0, j)),
        ],
        out_specs=pl.BlockSpec((None, r, tn), lambda i, j: (i, 0, j)),
        out_shape=jax.ShapeDtypeStruct((depth, r, n), F32),
        compiler_params=_params("parallel", "parallel"),
        name="modulation",
    )(c_all, w_mod, b_mod.reshape(depth, 1, n))


def _rope_angles(n_tokens, dp):
    t = jnp.arange(n_tokens, dtype=jnp.int32)
    inv = ROPE_BASE ** (-jnp.arange(0, dp, 2, dtype=F32) / dp)
    row = (t // GRID_W).astype(F32)[:, None] * inv[None, :]
    col = (t % GRID_W).astype(F32)[:, None] * inv[None, :]
    return row, col


def _rope_tables(n_tokens, head_dim, identity):
    half = head_dim // 2
    if identity:
        return jnp.ones((n_tokens, head_dim), F32), jnp.zeros((n_tokens, head_dim), F32)
    row, col = _rope_angles(n_tokens, half)
    cos = jnp.concatenate([jnp.cos(row), jnp.cos(row), jnp.cos(col), jnp.cos(col)], axis=-1)
    sin = jnp.concatenate([-jnp.sin(row), jnp.sin(row), -jnp.sin(col), jnp.sin(col)], axis=-1)
    return cos, sin


def _rope_apply(y, cos, sin, quarter):
    if 2 * quarter == V7X_LANES:
        return y * cos + pltpu.roll(y, quarter, 1) * sin
    lane = lax.broadcasted_iota(jnp.int32, y.shape, 1)
    first = (lane % (2 * quarter)) < quarter
    partner = jnp.where(first, pltpu.roll(y, V7X_LANES - quarter, 1), pltpu.roll(y, quarter, 1))
    return y * cos + partner * sin


def _proj_even_kernel(x_ref, sh_ref, sc_ref, w_ref, qg_ref, kg_ref, cos_ref, sin_ref,
                      f_ref, q_ref, k_ref, v_ref, *, fdim, qdim, kvdim, hd):
    u = _norm_mod(x_ref[...], sh_ref[...], sc_ref[...]).astype(BF16)
    cos = cos_ref[...]
    sin = sin_ref[...]

    def heads(lo, width, gain_ref, out_ref):
        seg = _tile(width, 512, hd)
        gain = gain_ref[...]
        for s in range(width // seg):
            y = jnp.dot(u, w_ref[:, lo + s * seg:lo + (s + 1) * seg], preferred_element_type=F32)
            for h in range(seg // hd):
                yh = y[:, h * hd:(h + 1) * hd]
                ms = jnp.mean(yh * yh, axis=-1, keepdims=True)
                yh = yh * lax.rsqrt(ms + EPS) * gain
                yh = _rope_apply(yh, cos, sin, hd // 4)
                out_ref[:, s * seg + h * hd:s * seg + (h + 1) * hd] = yh.astype(BF16)

    f_ref[...] = jnp.dot(u, w_ref[:, :fdim], preferred_element_type=F32).astype(BF16)
    heads(fdim, qdim, qg_ref, q_ref)
    heads(fdim + qdim, kvdim, kg_ref, k_ref)
    v_ref[...] = jnp.dot(u, w_ref[:, fdim + qdim + kvdim:], preferred_element_type=F32).astype(BF16)


def _proj_even(h, shift, scale, rows_per_mod, w, q_gain, k_gain, cos, sin, dims):
    n, d = h.shape
    fdim, qdim, kvdim = dims
    hd = ATTN_HEAD_DIM
    pos_rows = cos.shape[0]
    tm = _tile(math.gcd(rows_per_mod, pos_rows), 512)
    pos_tiles = pos_rows // tm
    kern = functools.partial(_proj_even_kernel, fdim=fdim, qdim=qdim, kvdim=kvdim, hd=hd)
    row = lambda width: pl.BlockSpec((tm, width), lambda i: (i, 0))
    const = lambda a: pl.BlockSpec(a.shape, lambda i: (0,) * a.ndim)
    return pl.pallas_call(
        kern,
        grid=(n // tm,),
        in_specs=[
            row(d), _mod_spec(shift, rows_per_mod, tm), _mod_spec(scale, rows_per_mod, tm),
            const(w), const(q_gain), const(k_gain),
            pl.BlockSpec((tm, hd), lambda i: (i % pos_tiles, 0)),
            pl.BlockSpec((tm, hd), lambda i: (i % pos_tiles, 0)),
        ],
        out_specs=[row(fdim), row(qdim), row(kvdim), row(kvdim)],
        out_shape=[jax.ShapeDtypeStruct((n, fdim), BF16), jax.ShapeDtypeStruct((n, qdim), BF16),
                   jax.ShapeDtypeStruct((n, kvdim), BF16), jax.ShapeDtypeStruct((n, kvdim), BF16)],
        compiler_params=_params("parallel"),
        name="proj_even",
    )(h, shift, scale, w, q_gain, k_gain, cos, sin)


def _attn_kernel(q_ref, *refs, n_seg, group, hd, scale, n_split):
    k_refs, v_refs = refs[:n_seg], refs[n_seg:2 * n_seg]
    o_ref, s_scr, p_scr, v_scr = refs[2 * n_seg:]
    tq = q_ref.shape[0]

    @pl.when(pl.program_id(2) == 0)
    def _():
        off = 0
        for v in v_refs:
            n = v.shape[0]
            v_scr[off:off + n, :hd] = v[...]
            v_scr[off:off + n, hd:] = jnp.ones((n, hd), BF16)
            off += n

    q = q_ref[...]
    qs = jnp.concatenate([q[:, g * hd:(g + 1) * hd] for g in range(group)], axis=0)
    nt = (((1,), (1,)), ((), ()))
    rows = qs.shape[0] // n_split
    blocks = [slice(i * rows, (i + 1) * rows) for i in range(n_split)]
    for blk in blocks:
        off = 0
        for k in k_refs:
            n = k.shape[0]
            s_scr[blk, off:off + n] = lax.dot_general(qs[blk], k[...], nt, preferred_element_type=F32)
            off += n
    outs = []
    for blk in blocks:
        m = jnp.max(s_scr[blk, :], axis=-1, keepdims=True)
        p_scr[blk, :] = jnp.exp2((s_scr[blk, :] - m) * (scale * math.log2(math.e))).astype(BF16)
    for blk in blocks:
        o = jnp.dot(p_scr[blk, :], v_scr[...], preferred_element_type=F32)
        outs.append(o[:, :hd] / o[:, hd:])
    o = jnp.concatenate(outs, axis=0)
    for g in range(group):
        o_ref[:, g * hd:(g + 1) * hd] = o[g * tq:(g + 1) * tq].astype(BF16)


def _attention(q, segments, batch):
    hd, kvh = ATTN_HEAD_DIM, ATTN_KV_HEADS
    n, qdim = q.shape
    group = qdim // (kvh * hd)
    t_q = n // batch
    tq = _tile(t_q, 512)
    nq = t_q // tq
    n_keys = sum(k.shape[0] for k, _ in segments) // batch
    kern = functools.partial(_attn_kernel, n_seg=len(segments), group=group, hd=hd, scale=hd ** -0.5,
                             n_split=4)
    kv_spec = lambda a: pl.BlockSpec((a.shape[0] // batch, hd), lambda b, kh, t: (b, kh))
    q_spec = pl.BlockSpec((tq, group * hd), lambda b, kh, t: (b * nq + t, kh))
    return pl.pallas_call(
        kern,
        grid=(batch, kvh, nq),
        in_specs=[q_spec] + [kv_spec(k) for k, _ in segments] + [kv_spec(v) for _, v in segments],
        out_specs=q_spec,
        out_shape=jax.ShapeDtypeStruct((n, qdim), BF16),
        scratch_shapes=[pltpu.VMEM((group * tq, n_keys), F32), pltpu.VMEM((group * tq, n_keys), BF16),
                        pltpu.VMEM((n_keys, 2 * hd), BF16)],
        compiler_params=_params("parallel", "parallel", "arbitrary"),
        name="attention",
    )(q, *[k for k, _ in segments], *[v for _, v in segments])


def _fourier_kernel(u_ref, cm_ref, cs_ref, o_ref, z_scr, *, groups, gd, scale):
    t = u_ref.shape[0]

    @pl.when(pl.program_id(1) == 0)
    def _():
        for g in range(groups):
            cols = slice(g * gd, (g + 1) * gd)
            z = jnp.dot(u_ref[:, cols], cm_ref[...], preferred_element_type=F32)
            z_scr[0:t, cols] = z[:, :gd].astype(BF16)
            z_scr[t:2 * t, cols] = z[:, gd:].astype(BF16)

    y = jnp.dot(cs_ref[...], z_scr[...], preferred_element_type=F32)
    o_ref[...] = (y * scale).astype(BF16)


def _dft_tables(t, gd):
    def angles(n):
        i = jnp.arange(n, dtype=jnp.int32)
        return ((i[:, None] * i[None, :]) % n).astype(F32) * (2.0 * math.pi / n)
    a_c, a_t = angles(gd), angles(t)
    cm = jnp.concatenate([jnp.cos(a_c), -jnp.sin(a_c)], axis=1).astype(BF16)
    cs = jnp.concatenate([jnp.cos(a_t), jnp.sin(a_t)], axis=1).astype(BF16)
    return cm, cs


def _fourier(u, batch, tables):
    n, fdim = u.shape
    t = n // batch
    gd = fdim // FOURIER_GROUPS
    cm, cs = tables
    tr = _tile(t, 512)
    kern = functools.partial(_fourier_kernel, groups=FOURIER_GROUPS, gd=gd, scale=(t * gd) ** -0.5)
    return pl.pallas_call(
        kern,
        grid=(batch, t // tr),
        in_specs=[
            pl.BlockSpec((t, fdim), lambda b, r: (b, 0)),
            pl.BlockSpec(cm.shape, lambda b, r: (0, 0)),
            pl.BlockSpec((tr, 2 * t), lambda b, r: (r, 0)),
        ],
        out_specs=pl.BlockSpec((tr, fdim), lambda b, r: (b * (t // tr) + r, 0)),
        out_shape=jax.ShapeDtypeStruct((n, fdim), BF16),
        scratch_shapes=[pltpu.VMEM((2 * t, fdim), BF16)],
        compiler_params=_params("parallel", "arbitrary"),
        name="fourier",
    )(u, cm, cs)


def _proj_odd_kernel(x_ref, sh_ref, sc_ref, w_ref, cos_ref, sin_ref, o_ref, u_scr,
                     *, q_tiles, qk_tiles, hd, k_scale):
    j = pl.program_id(1)

    @pl.when(j == 0)
    def _():
        u_scr[...] = _norm_mod(x_ref[...], sh_ref[...], sc_ref[...]).astype(BF16)

    y = jnp.dot(u_scr[...], w_ref[...], preferred_element_type=F32)
    tn = y.shape[1]

    @pl.when(j < qk_tiles)
    def _():
        mult = jnp.where(j < q_tiles, 1.0, k_scale).astype(F32)
        for c in range(tn // V7X_LANES):
            lanes = slice(c * V7X_LANES, (c + 1) * V7X_LANES)
            tab = slice((c % (hd // V7X_LANES)) * V7X_LANES, (c % (hd // V7X_LANES) + 1) * V7X_LANES)
            r = _rope_apply(y[:, lanes], cos_ref[:, tab], sin_ref[:, tab], hd // 4)
            o_ref[:, lanes] = (r * mult).astype(BF16)

    @pl.when(j >= qk_tiles)
    def _():
        o_ref[...] = y.astype(BF16)


def _proj_odd(h, shift, scale, rows_per_mod, w, cos, sin, qk_dim):
    n, d = h.shape
    n_out = w.shape[1]
    hd = cos.shape[1]
    pos_rows = cos.shape[0]
    tm = _tile(math.gcd(rows_per_mod, pos_rows), 1024)
    tn = _tile(math.gcd(qk_dim, n_out), 1024, hd)
    pos_tiles = pos_rows // tm
    kern = functools.partial(_proj_odd_kernel, q_tiles=qk_dim // tn, qk_tiles=2 * qk_dim // tn,
                             hd=hd, k_scale=hd ** -0.5)
    return pl.pallas_call(
        kern,
        grid=(n // tm, n_out // tn),
        in_specs=[
            pl.BlockSpec((tm, d), lambda i, j: (i, 0)),
            _mod_spec(shift, rows_per_mod, tm, 1), _mod_spec(scale, rows_per_mod, tm, 1),
            pl.BlockSpec((d, tn), lambda i, j: (0, j)),
            pl.BlockSpec((tm, hd), lambda i, j: (i % pos_tiles, 0)),
            pl.BlockSpec((tm, hd), lambda i, j: (i % pos_tiles, 0)),
        ],
        out_specs=pl.BlockSpec((tm, tn), lambda i, j: (i, j)),
        out_shape=jax.ShapeDtypeStruct((n, n_out), BF16),
        scratch_shapes=[pltpu.VMEM((tm, d), BF16)],
        compiler_params=_params("parallel", "arbitrary"),
        name="proj_odd",
    )(h, shift, scale, w, cos, sin)


def _retention_kernel(lgf_ref, lgb_ref, qc_ref, kc_ref, vc_ref, gc_ref, ql_ref, kl_ref, vl_ref, gl_ref,
                      oc_ref, ol_ref, sf_scr, sb_scr, d_scr, dec_scr, partc_scr, partl_scr, *, chunk):
    c = chunk
    dk = sf_scr.shape[0]
    lgf = lgf_ref[...][:, :1]
    lgb = lgb_ref[...][:, :1]
    diff = (lax.broadcasted_iota(jnp.int32, (c, c), 0)
            - lax.broadcasted_iota(jnp.int32, (c, c), 1)).astype(F32)
    d_scr[...] = jnp.where(diff >= 0.0, jnp.exp(jnp.maximum(diff, 0.0) * lgf),
                           jnp.exp(jnp.maximum(-diff, 0.0) * lgb))
    idx = lax.broadcasted_iota(jnp.int32, (c, dk), 0).astype(F32)
    dec_scr[0] = jnp.exp((idx + 1.0) * lgf).astype(BF16)
    dec_scr[1] = jnp.exp((c - 1.0 - idx) * lgf).astype(BF16)
    dec_scr[2] = jnp.exp((c - idx) * lgb).astype(BF16)
    dec_scr[3] = jnp.exp(idx * lgb).astype(BF16)
    chunk_dec_f, chunk_dec_b = jnp.exp(c * lgf), jnp.exp(c * lgb)
    nt = (((1,), (1,)), ((), ()))
    tn = (((0,), (0,)), ((), ()))

    sf_scr[...] = jnp.zeros_like(sf_scr)
    sb_scr[...] = jnp.zeros_like(sb_scr)

    def forward(seg, i):
        q_ref, k_ref, v_ref = seg[:3]
        rows = slice(i * c, (i + 1) * c)
        q, k, v = q_ref[rows, :], k_ref[rows, :], v_ref[rows, :]
        s = lax.dot_general(q, k, nt, preferred_element_type=F32) * d_scr[...]
        lhs = jnp.concatenate([s.astype(BF16), q * dec_scr[0]], axis=1)
        rhs = jnp.concatenate([v, sf_scr[...].astype(BF16)], axis=0)
        o = jnp.dot(lhs, rhs, preferred_element_type=F32)
        sf_scr[...] = sf_scr[...] * chunk_dec_f + lax.dot_general(k * dec_scr[1], v, tn,
                                                                  preferred_element_type=F32)
        return o

    def backward(seg, i):
        q_ref, k_ref, v_ref = seg[:3]
        rows = slice(i * c, (i + 1) * c)
        q, k, v = q_ref[rows, :], k_ref[rows, :], v_ref[rows, :]
        o = jnp.dot(q * dec_scr[2], sb_scr[...].astype(BF16), preferred_element_type=F32)
        sb_scr[...] = sb_scr[...] * chunk_dec_b + lax.dot_general(k * dec_scr[3], v, tn,
                                                                  preferred_element_type=F32)
        return o

    def finish(seg, i, o):
        g_ref, o_ref = seg[3], seg[4]
        rows = slice(i * c, (i + 1) * c)
        o = o * lax.rsqrt(jnp.mean(o * o, axis=-1, keepdims=True) + EPS)
        o_ref[rows, :] = _silu(g_ref[rows, :]) * o.astype(BF16)

    def sweep(seg):
        part = seg[5]
        n = seg[0].shape[0] // c
        for t in range(n):
            i_f, i_b = t, n - 1 - t
            o_f, o_b = forward(seg, i_f), backward(seg, i_b)
            if i_f == i_b:
                finish(seg, i_f, o_f + o_b)
            elif i_f < i_b:
                part[i_f * c:(i_f + 1) * c, :] = o_f
                part[i_b * c:(i_b + 1) * c, :] = o_b
            else:
                finish(seg, i_f, o_f + part[i_f * c:(i_f + 1) * c, :])
                finish(seg, i_b, o_b + part[i_b * c:(i_b + 1) * c, :])

    sweep((qc_ref, kc_ref, vc_ref, gc_ref, oc_ref, partc_scr))
    sweep((ql_ref, kl_ref, vl_ref, gl_ref, ol_ref, partl_scr))


def _retention(p_ctx, p_lat, lg_fwd, lg_bwd, batch):
    heads = lg_fwd.shape[0]
    n_out = p_lat.shape[1]
    dk = n_out // (6 * heads)
    dv = 2 * dk
    t_c, t_l = p_ctx.shape[0] // batch, p_lat.shape[0] // batch
    c = _tile(math.gcd(t_c, t_l), 256)
    lanes = lambda lg: jnp.broadcast_to(lg.astype(F32)[:, None, None], (heads, 1, V7X_LANES))
    lg_spec = pl.BlockSpec((None, 1, V7X_LANES), lambda b, h: (h, 0, 0))

    def specs(t):
        return [pl.BlockSpec((t, dk), lambda b, h: (b, h)),
                pl.BlockSpec((t, dk), lambda b, h: (b, heads + h)),
                pl.BlockSpec((t, dv), lambda b, h: (b, heads + h)),
                pl.BlockSpec((t, dv), lambda b, h: (b, 2 * heads + h))]

    out_spec = lambda t: pl.BlockSpec((t, dv), lambda b, h: (b, h))
    return pl.pallas_call(
        functools.partial(_retention_kernel, chunk=c),
        grid=(batch, heads),
        in_specs=[lg_spec, lg_spec] + specs(t_c) + specs(t_l),
        out_specs=[out_spec(t_c), out_spec(t_l)],
        out_shape=[jax.ShapeDtypeStruct((p_ctx.shape[0], heads * dv), BF16),
                   jax.ShapeDtypeStruct((p_lat.shape[0], heads * dv), BF16)],
        scratch_shapes=[pltpu.VMEM((dk, dv), F32), pltpu.VMEM((dk, dv), F32), pltpu.VMEM((c, c), F32),
                        pltpu.VMEM((4, c, dk), BF16),
                        pltpu.VMEM((t_c, dv), F32), pltpu.VMEM((t_l, dv), F32)],
        compiler_params=_params("parallel", "parallel"),
        name="retention",
    )(lanes(lg_fwd), lanes(lg_bwd), p_ctx, p_ctx, p_ctx, p_ctx, p_lat, p_lat, p_lat, p_lat)


def _outproj_kernel(*refs, n_parts):
    a_refs, w_refs = refs[:n_parts], refs[n_parts:2 * n_parts]
    res_ref, gate_ref, o_ref = refs[2 * n_parts:]
    acc = functools.reduce(jnp.add, [jnp.dot(a[...], w[...], preferred_element_type=F32)
                                     for a, w in zip(a_refs, w_refs)])
    o_ref[...] = res_ref[...] + gate_ref[...] * acc


def _outproj(parts, weights, res, gate, rows_per_mod):
    n, d = res.shape
    tm = _tile(rows_per_mod, 1024)
    tn = _tile(d, 512, V7X_LANES)
    tiles = rows_per_mod // tm
    return pl.pallas_call(
        functools.partial(_outproj_kernel, n_parts=len(parts)),
        grid=(n // tm, d // tn),
        in_specs=[pl.BlockSpec((tm, a.shape[1]), lambda i, j: (i, 0)) for a in parts]
        + [pl.BlockSpec((w.shape[0], tn), lambda i, j: (0, j)) for w in weights]
        + [pl.BlockSpec((tm, tn), lambda i, j: (i, j)),
           pl.BlockSpec((None, 1, tn), lambda i, j: (i // tiles, 0, j))],
        out_specs=pl.BlockSpec((tm, tn), lambda i, j: (i, j)),
        out_shape=jax.ShapeDtypeStruct((n, d), F32),
        compiler_params=_params("parallel", "parallel"),
        name="outproj",
    )(*parts, *weights, res, gate)


def _ffn_kernel(x_ref, sh_ref, sc_ref, gate_ref, wg_ref, wu_ref, wo_ref, o_ref, u_scr, acc_scr):
    j = pl.program_id(1)

    @pl.when(j == 0)
    def _():
        u_scr[...] = _norm_mod(x_ref[...], sh_ref[...], sc_ref[...]).astype(BF16)
        acc_scr[...] = jnp.zeros_like(acc_scr)

    u = u_scr[...]
    g = jnp.dot(u, wg_ref[...], preferred_element_type=F32)
    up = jnp.dot(u, wu_ref[...], preferred_element_type=F32)
    act = (_silu(g) * up).astype(BF16)
    acc_scr[...] += jnp.dot(act, wo_ref[...], preferred_element_type=F32)

    @pl.when(j == pl.num_programs(1) - 1)
    def _():
        o_ref[...] = x_ref[...] + gate_ref[...] * acc_scr[...]


def _ffn(h, shift, scale, gate, rows_per_mod, w_in, w_out):
    n, d = h.shape
    f = w_out.shape[0]
    tm = _tile(rows_per_mod, 512)
    tf = _tile(f, 512, V7X_LANES)
    nf = f // tf
    return pl.pallas_call(
        _ffn_kernel,
        grid=(n // tm, nf),
        in_specs=[
            pl.BlockSpec((tm, d), lambda i, j: (i, 0)),
            _mod_spec(shift, rows_per_mod, tm, 1), _mod_spec(scale, rows_per_mod, tm, 1),
            _mod_spec(gate, rows_per_mod, tm, 1),
            pl.BlockSpec((d, tf), lambda i, j: (0, j)),
            pl.BlockSpec((d, tf), lambda i, j: (0, nf + j)),
            pl.BlockSpec((tf, d), lambda i, j: (j, 0)),
        ],
        out_specs=pl.BlockSpec((tm, d), lambda i, j: (i, 0)),
        out_shape=jax.ShapeDtypeStruct((n, d), F32),
        scratch_shapes=[pltpu.VMEM((tm, d), BF16), pltpu.VMEM((tm, d), F32)],
        compiler_params=_params("parallel", "arbitrary"),
        name="ffn",
    )(h, shift, scale, gate, w_in, w_in, w_out)


def _even_layer(h_ctx, h_lat, mod_ctx, mod_lat, w_in, w_out, q_gain, k_gain, batch, need_ctx, tables):
    n_ctx, d = h_ctx.shape
    n_lat = h_lat.shape[0]
    hd = ATTN_HEAD_DIM
    fdim = FOURIER_GROUPS * (d // 16)
    kvdim = ATTN_KV_HEADS * hd
    qdim = w_in.shape[1] - fdim - 2 * kvdim
    dims = (fdim, qdim, kvdim)
    w_in = w_in.astype(BF16)
    w_f, w_a = w_out[:fdim].astype(BF16), w_out[fdim:].astype(BF16)
    q_gain, k_gain = q_gain.reshape(1, hd), k_gain.reshape(1, hd)

    f_c, q_c, k_c, v_c = _proj_even(h_ctx, mod_ctx[0], mod_ctx[1], n_ctx, w_in, q_gain, k_gain,
                                    *tables["rope_even_ctx"], dims)
    f_l, q_l, k_l, v_l = _proj_even(h_lat, mod_lat[0], mod_lat[1], n_lat // batch, w_in, q_gain, k_gain,
                                    *tables["rope_even_lat"], dims)
    a_l = _attention(q_l, [(k_l, v_l), (k_c, v_c)], batch)
    m_l = _fourier(f_l, batch, tables["dft_lat"])
    h_lat = _outproj([m_l, a_l], [w_f, w_a], h_lat, mod_lat[2], n_lat // batch)
    if need_ctx:
        a_c = _attention(q_c, [(k_c, v_c)], batch)
        m_c = _fourier(f_c, batch, tables["dft_ctx"])
        h_ctx = _outproj([m_c, a_c], [w_f, w_a], h_ctx, mod_ctx[2], n_ctx)
    return h_ctx, h_lat


def _odd_layer(h_ctx, h_lat, mod_ctx, mod_lat, w_in, w_out, lg_fwd, lg_bwd, batch, need_ctx, tables):
    n_ctx = h_ctx.shape[0]
    n_lat = h_lat.shape[0]
    heads = lg_fwd.shape[0]
    qk_dim = w_in.shape[1] // 6
    w_in, w_out = w_in.astype(BF16), w_out.astype(BF16)
    p_c = _proj_odd(h_ctx, mod_ctx[0], mod_ctx[1], n_ctx, w_in, *tables["rope_odd_ctx"], qk_dim)
    p_l = _proj_odd(h_lat, mod_lat[0], mod_lat[1], n_lat // batch, w_in, *tables["rope_odd_lat"], qk_dim)
    m_c, m_l = _retention(p_c, p_l, lg_fwd, lg_bwd, batch)
    h_lat = _outproj([m_l], [w_out], h_lat, mod_lat[2], n_lat // batch)
    if need_ctx:
        h_ctx = _outproj([m_c], [w_out], h_ctx, mod_ctx[2], n_ctx)
    return h_ctx, h_lat


def kernel(x, c, ctx, c_ctx, w_mod, b_mod, w_in_even, w_out_even, q_gain_even, k_gain_even,
           w_in_odd, w_out_odd, log_decay_fwd, log_decay_bwd, w_ffn_in, w_ffn_out):
    batch, seq, d = x.shape
    ctx_len = ctx.shape[1]
    depth = w_mod.shape[0]
    n_lat, n_ctx = batch * seq, batch * ctx_len
    h_lat, h_ctx = x.reshape(n_lat, d), ctx.reshape(n_ctx, d)

    rows = -(-(batch + 1) // 8) * 8
    c_all = jnp.concatenate([c, c_ctx[None], jnp.zeros((rows - batch - 1, d), F32)], axis=0)
    mods = _modulation(c_all, w_mod, b_mod).reshape(depth, rows, N_MOD, d)

    ret_hd = w_in_odd.shape[2] // (6 * log_decay_fwd.shape[1])
    ctx_rows = _tile(n_ctx, 1024)
    tables = {
        "rope_even_lat": _rope_tables(seq, ATTN_HEAD_DIM, False),
        "rope_even_ctx": _rope_tables(ctx_rows, ATTN_HEAD_DIM, True),
        "rope_odd_lat": _rope_tables(seq, ret_hd, False),
        "rope_odd_ctx": _rope_tables(ctx_rows, ret_hd, True),
        "dft_lat": _dft_tables(seq, d // 16),
        "dft_ctx": _dft_tables(ctx_len, d // 16),
    }

    for i in range(depth):
        need_ctx = i < depth - 1
        mod_lat = [mods[i, :batch, k][:, None, :] for k in range(N_MOD)]
        mod_ctx = [mods[i, batch:batch + 1, k][:, None, :] for k in range(N_MOD)]
        j = i // 2
        if i % 2 == 0:
            h_ctx, h_lat = _even_layer(h_ctx, h_lat, mod_ctx, mod_lat, w_in_even[j], w_out_even[j],
                                       q_gain_even[j], k_gain_even[j], batch, need_ctx, tables)
        else:
            h_ctx, h_lat = _odd_layer(h_ctx, h_lat, mod_ctx, mod_lat, w_in_odd[j], w_out_odd[j],
                                      log_decay_fwd[j], log_decay_bwd[j], batch, need_ctx, tables)
        w_in, w_out = w_ffn_in[i].astype(BF16), w_ffn_out[i].astype(BF16)
        h_lat = _ffn(h_lat, mod_lat[3], mod_lat[4], mod_lat[5], n_lat // batch, w_in, w_out)
        if need_ctx:
            h_ctx = _ffn(h_ctx, mod_ctx[3], mod_ctx[4], mod_ctx[5], n_ctx, w_in, w_out)
    return h_lat.reshape(batch, seq, d)
```

```python
import functools
import math

import jax
import jax.numpy as jnp
from jax import lax
from jax.experimental import pallas as pl
from jax.experimental.pallas import tpu as pltpu

F32 = jnp.float32
BF16 = jnp.bfloat16

EPS = 1e-6
ROPE_BASE = 10000.0
GRID_W = 64
N_MOD = 6
FOURIER_GROUPS = 4
ATTN_HEAD_DIM = 128
ATTN_KV_HEADS = 4

V7X_LANES = 128
V7X_VMEM_BYTES = 64 * 1024 * 1024
VMEM_LIMIT_BYTES = V7X_VMEM_BYTES - 8 * 1024 * 1024


def _tile(n, target, align=8):
    best = None
    for t in range(align, min(n, target) + 1, align):
        if n % t == 0:
            best = t
    return n if best is None else best


def _params(*semantics):
    return pltpu.CompilerParams(dimension_semantics=semantics, vmem_limit_bytes=VMEM_LIMIT_BYTES)


def _silu(x):
    return x * jax.nn.sigmoid(x)


def _norm_mod(x, shift, scale):
    ms = jnp.mean(x * x, axis=-1, keepdims=True)
    return x * lax.rsqrt(ms + EPS) * (1.0 + scale) + shift


def _next_tile(i, n_tiles):
    return jnp.minimum(i + 1, n_tiles - 1)


def _row_spec(tm, width, n_tiles=None):
    if n_tiles is None:
        return pl.BlockSpec((tm, width), lambda i, *_: (i, 0))
    return pl.BlockSpec((tm, width), lambda i, *_: (_next_tile(i, n_tiles), 0))


def _mod_spec(mod, rows_per_mod, tm, n_tiles=None):
    tiles = rows_per_mod // tm
    d = mod.shape[-1]
    if n_tiles is None:
        return pl.BlockSpec((None, 1, d), lambda i, *_: (i // tiles, 0, 0))
    return pl.BlockSpec((None, 1, d), lambda i, *_: (_next_tile(i, n_tiles) // tiles, 0, 0))


def _stage_modulated(x_ref, sh_ref, sc_ref, u_scr, slot, step, n_steps):
    tm = x_ref.shape[0]
    if n_steps == 1:
        rows = slice(None)
    else:
        r = min(tm, -(-(-(-tm // n_steps)) // 16) * 16)
        rows = pl.ds(pl.multiple_of(jnp.minimum(step * r, tm - r), 16), r)
    u_scr[slot, rows, :] = _norm_mod(x_ref[rows, :], sh_ref[...], sc_ref[...]).astype(BF16)


def _mod_kernel(c_ref, w_ref, b_ref, o_ref):
    cond = _silu(c_ref[...]).astype(BF16)
    o_ref[...] = jnp.dot(cond, w_ref[...].astype(BF16), preferred_element_type=F32) + b_ref[...]


def _modulation(c_all, w_mod, b_mod):
    depth, d, n = w_mod.shape
    r = c_all.shape[0]
    tn = _tile(n, 1024, V7X_LANES)
    return pl.pallas_call(
        _mod_kernel,
        grid=(depth, n // tn),
        in_specs=[
            pl.BlockSpec((r, d), lambda i, j: (0, 0)),
            pl.BlockSpec((None, d, tn), lambda i, j: (i, 0, j)),
            pl.BlockSpec((None, 1, tn), lambda i, j: (i, 0, j)),
        ],
        out_specs=pl.BlockSpec((None, r, tn), lambda i, j: (i, 0, j)),
        out_shape=jax.ShapeDtypeStruct((depth, r, n), F32),
        compiler_params=_params("parallel", "parallel"),
        name="modulation",
    )(c_all, w_mod, b_mod.reshape(depth, 1, n))


def _cast_kernel(x_ref, o_ref):
    o_ref[...] = x_ref[...].astype(BF16)


def _to_bf16(w, row_lo=0, row_hi=None):
    layers, k, n = w.shape
    row_hi = k if row_hi is None else row_hi
    rows = row_hi - row_lo
    tk = _tile(math.gcd(rows, row_lo), 256)
    tn = _tile(n, 4096, V7X_LANES)
    first = row_lo // tk
    return pl.pallas_call(
        _cast_kernel,
        grid=(layers, rows // tk, n // tn),
        in_specs=[pl.BlockSpec((None, tk, tn), lambda l, r, c: (l, first + r, c))],
        out_specs=pl.BlockSpec((None, tk, tn), lambda l, r, c: (l, r, c)),
        out_shape=jax.ShapeDtypeStruct((layers, rows, n), BF16),
        compiler_params=_params("parallel", "parallel", "parallel"),
        name="to_bf16",
    )(w)


def _rope_angles(n_tokens, dp):
    t = jnp.arange(n_tokens, dtype=jnp.int32)
    inv = ROPE_BASE ** (-jnp.arange(0, dp, 2, dtype=F32) / dp)
    row = (t // GRID_W).astype(F32)[:, None] * inv[None, :]
    col = (t % GRID_W).astype(F32)[:, None] * inv[None, :]
    return row, col


def _rope_tables(n_tokens, head_dim, identity):
    half = head_dim // 2
    if identity:
        return jnp.ones((n_tokens, head_dim), F32), jnp.zeros((n_tokens, head_dim), F32)
    row, col = _rope_angles(n_tokens, half)
    cos = jnp.concatenate([jnp.cos(row), jnp.cos(row), jnp.cos(col), jnp.cos(col)], axis=-1)
    sin = jnp.concatenate([-jnp.sin(row), jnp.sin(row), -jnp.sin(col), jnp.sin(col)], axis=-1)
    return cos, sin


def _rope_apply(y, cos, sin, quarter):
    if 2 * quarter == V7X_LANES:
        return y * cos + pltpu.roll(y, quarter, 1) * sin
    lane = lax.broadcasted_iota(jnp.int32, y.shape, 1)
    first = (lane % (2 * quarter)) < quarter
    partner = jnp.where(first, pltpu.roll(y, V7X_LANES - quarter, 1), pltpu.roll(y, quarter, 1))
    return y * cos + partner * sin


def _proj_even_kernel(x_ref, sh_ref, sc_ref, xn_ref, shn_ref, scn_ref, w_ref, qg_ref, kg_ref, cos_ref, sin_ref,
                      f_ref, q_ref, k_ref, v_ref, u_scr, *, fdim, qdim, kvdim, hd):
    i = pl.program_id(0)
    slot = i % 2

    @pl.when(i == 0)
    def _():
        _stage_modulated(x_ref, sh_ref, sc_ref, u_scr, 0, 0, 1)

    u = u_scr[slot]
    cos = cos_ref[...]
    sin = sin_ref[...]

    def heads(lo, width, gain_ref, out_ref):
        seg = _tile(width, 512, hd)
        gain = gain_ref[...]
        for s in range(width // seg):
            y = jnp.dot(u, w_ref[:, lo + s * seg:lo + (s + 1) * seg], preferred_element_type=F32)
            for h in range(seg // hd):
                yh = y[:, h * hd:(h + 1) * hd]
                ms = jnp.mean(yh * yh, axis=-1, keepdims=True)
                yh = yh * lax.rsqrt(ms + EPS) * gain
                yh = _rope_apply(yh, cos, sin, hd // 4)
                out_ref[:, s * seg + h * hd:s * seg + (h + 1) * hd] = yh.astype(BF16)

    heads(fdim, qdim, qg_ref, q_ref)
    heads(fdim + qdim, kvdim, kg_ref, k_ref)
    f_ref[...] = jnp.dot(u, w_ref[:, :fdim], preferred_element_type=F32).astype(BF16)
    v_ref[...] = jnp.dot(u, w_ref[:, fdim + qdim + kvdim:], preferred_element_type=F32).astype(BF16)
    _stage_modulated(xn_ref, shn_ref, scn_ref, u_scr, 1 - slot, 0, 1)


def _first_tile_specs(tm, d, shift, scale):
    once = pl.Buffered(1)
    return [pl.BlockSpec((tm, d), lambda i, *_: (0, 0), pipeline_mode=once),
            pl.BlockSpec((None, 1, d), lambda i, *_: (0, 0, 0), pipeline_mode=once),
            pl.BlockSpec((None, 1, d), lambda i, *_: (0, 0, 0), pipeline_mode=once)]


def _proj_even(h, shift, scale, rows_per_mod, w, layer, q_gain, k_gain, cos, sin, dims):
    n, d = h.shape
    fdim, qdim, kvdim = dims
    hd = ATTN_HEAD_DIM
    pos_rows = cos.shape[0]
    tm = _tile(math.gcd(rows_per_mod, pos_rows), 512)
    n_tiles = n // tm
    pos_tiles = pos_rows // tm
    kern = functools.partial(_proj_even_kernel, fdim=fdim, qdim=qdim, kvdim=kvdim, hd=hd)
    const = lambda a: pl.BlockSpec(a.shape, lambda i: (0,) * a.ndim)
    return pl.pallas_call(
        kern,
        grid=(n_tiles,),
        in_specs=_first_tile_specs(tm, d, shift, scale) + [
            _row_spec(tm, d, n_tiles),
            _mod_spec(shift, rows_per_mod, tm, n_tiles), _mod_spec(scale, rows_per_mod, tm, n_tiles),
            pl.BlockSpec((None,) + w.shape[1:], lambda i: (layer, 0, 0)),
            const(q_gain), const(k_gain),
            pl.BlockSpec((tm, hd), lambda i: (i % pos_tiles, 0)),
            pl.BlockSpec((tm, hd), lambda i: (i % pos_tiles, 0)),
        ],
        out_specs=[_row_spec(tm, fdim), _row_spec(tm, qdim), _row_spec(tm, kvdim), _row_spec(tm, kvdim)],
        out_shape=[jax.ShapeDtypeStruct((n, fdim), BF16), jax.ShapeDtypeStruct((n, qdim), BF16),
                   jax.ShapeDtypeStruct((n, kvdim), BF16), jax.ShapeDtypeStruct((n, kvdim), BF16)],
        scratch_shapes=[pltpu.VMEM((2, tm, d), BF16)],
        compiler_params=_params("arbitrary"),
        name="proj_even",
    )(h, shift, scale, h, shift, scale, w, q_gain, k_gain, cos, sin)


def _attn_kernel(q_ref, *refs, n_seg, group, hd, scale, n_split):
    k_refs, v_refs = refs[:n_seg], refs[n_seg:2 * n_seg]
    o_ref, s_scr, p_scr, v_scr = refs[2 * n_seg:]
    tq = q_ref.shape[0]

    @pl.when(pl.program_id(2) == 0)
    def _():
        off = 0
        for v in v_refs:
            n = v.shape[0]
            v_scr[off:off + n, :hd] = v[...]
            v_scr[off:off + n, hd:] = jnp.ones((n, hd), BF16)
            off += n

    q = q_ref[...]
    qs = jnp.concatenate([q[:, g * hd:(g + 1) * hd] for g in range(group)], axis=0)
    nt = (((1,), (1,)), ((), ()))
    rows = qs.shape[0] // n_split
    blocks = [slice(i * rows, (i + 1) * rows) for i in range(n_split)]
    for blk in blocks:
        off = 0
        for k in k_refs:
            n = k.shape[0]
            s_scr[blk, off:off + n] = lax.dot_general(qs[blk], k[...], nt, preferred_element_type=F32)
            off += n
    outs = []
    for blk in blocks:
        m = jnp.max(s_scr[blk, :], axis=-1, keepdims=True)
        p_scr[blk, :] = jnp.exp2((s_scr[blk, :] - m) * (scale * math.log2(math.e))).astype(BF16)
    for blk in blocks:
        o = jnp.dot(p_scr[blk, :], v_scr[...], preferred_element_type=F32)
        outs.append(o[:, :hd] / o[:, hd:])
    o = jnp.concatenate(outs, axis=0)
    for g in range(group):
        o_ref[:, g * hd:(g + 1) * hd] = o[g * tq:(g + 1) * tq].astype(BF16)


def _attention(q, segments, batch):
    hd, kvh = ATTN_HEAD_DIM, ATTN_KV_HEADS
    n, qdim = q.shape
    group = qdim // (kvh * hd)
    t_q = n // batch
    tq = _tile(t_q, 512)
    nq = t_q // tq
    n_keys = sum(k.shape[0] for k, _ in segments) // batch
    kern = functools.partial(_attn_kernel, n_seg=len(segments), group=group, hd=hd, scale=hd ** -0.5,
                             n_split=4)
    kv_spec = lambda a: pl.BlockSpec((a.shape[0] // batch, hd), lambda b, kh, t: (b, kh))
    q_spec = pl.BlockSpec((tq, group * hd), lambda b, kh, t: (b * nq + t, kh))
    return pl.pallas_call(
        kern,
        grid=(batch, kvh, nq),
        in_specs=[q_spec] + [kv_spec(k) for k, _ in segments] + [kv_spec(v) for _, v in segments],
        out_specs=q_spec,
        out_shape=jax.ShapeDtypeStruct((n, qdim), BF16),
        scratch_shapes=[pltpu.VMEM((group * tq, n_keys), F32), pltpu.VMEM((group * tq, n_keys), BF16),
                        pltpu.VMEM((n_keys, 2 * hd), BF16)],
        compiler_params=_params("parallel", "parallel", "arbitrary"),
        name="attention",
    )(q, *[k for k, _ in segments], *[v for _, v in segments])


def _fourier_kernel(u_ref, cm_ref, cs_ref, o_ref, z_scr, *, groups, gd, scale):
    t = u_ref.shape[0]

    @pl.when(pl.program_id(1) == 0)
    def _():
        for g in range(groups):
            cols = slice(g * gd, (g + 1) * gd)
            z = jnp.dot(u_ref[:, cols], cm_ref[...], preferred_element_type=F32)
            z_scr[0:t, cols] = z[:, :gd].astype(BF16)
            z_scr[t:2 * t, cols] = z[:, gd:].astype(BF16)

    y = jnp.dot(cs_ref[...], z_scr[...], preferred_element_type=F32)
    o_ref[...] = (y * scale).astype(BF16)


def _dft_tables(t, gd):
    def angles(n):
        i = jnp.arange(n, dtype=jnp.int32)
        return ((i[:, None] * i[None, :]) % n).astype(F32) * (2.0 * math.pi / n)
    a_c, a_t = angles(gd), angles(t)
    cm = jnp.concatenate([jnp.cos(a_c), -jnp.sin(a_c)], axis=1).astype(BF16)
    cs = jnp.concatenate([jnp.cos(a_t), jnp.sin(a_t)], axis=1).astype(BF16)
    return cm, cs


def _fourier(u, batch, tables):
    n, fdim = u.shape
    t = n // batch
    gd = fdim // FOURIER_GROUPS
    cm, cs = tables
    tr = _tile(t, 512)
    kern = functools.partial(_fourier_kernel, groups=FOURIER_GROUPS, gd=gd, scale=(t * gd) ** -0.5)
    return pl.pallas_call(
        kern,
        grid=(batch, t // tr),
        in_specs=[
            pl.BlockSpec((t, fdim), lambda b, r: (b, 0)),
            pl.BlockSpec(cm.shape, lambda b, r: (0, 0)),
            pl.BlockSpec((tr, 2 * t), lambda b, r: (r, 0)),
        ],
        out_specs=pl.BlockSpec((tr, fdim), lambda b, r: (b * (t // tr) + r, 0)),
        out_shape=jax.ShapeDtypeStruct((n, fdim), BF16),
        scratch_shapes=[pltpu.VMEM((2 * t, fdim), BF16)],
        compiler_params=_params("parallel", "arbitrary"),
        name="fourier",
    )(u, cm, cs)


def _proj_odd_kernel(x_ref, sh_ref, sc_ref, xn_ref, shn_ref, scn_ref, w_ref, cos_ref, sin_ref, o_ref, u_scr,
                     *, n_steps, q_tiles, qk_tiles, hd, k_scale):
    i, j = pl.program_id(0), pl.program_id(1)
    slot = i % 2

    @pl.when((i == 0) & (j == 0))
    def _():
        _stage_modulated(x_ref, sh_ref, sc_ref, u_scr, 0, 0, 1)

    y = jnp.dot(u_scr[slot], w_ref[...], preferred_element_type=F32)
    tn = y.shape[1]
    _stage_modulated(xn_ref, shn_ref, scn_ref, u_scr, 1 - slot, j, n_steps)

    @pl.when(j < qk_tiles)
    def _():
        mult = jnp.where(j < q_tiles, 1.0, k_scale).astype(F32)
        for c in range(tn // V7X_LANES):
            lanes = slice(c * V7X_LANES, (c + 1) * V7X_LANES)
            tab = slice((c % (hd // V7X_LANES)) * V7X_LANES, (c % (hd // V7X_LANES) + 1) * V7X_LANES)
            r = _rope_apply(y[:, lanes], cos_ref[:, tab], sin_ref[:, tab], hd // 4)
            o_ref[:, lanes] = (r * mult).astype(BF16)

    @pl.when(j >= qk_tiles)
    def _():
        o_ref[...] = y.astype(BF16)


def _proj_odd(h, shift, scale, rows_per_mod, w, layer, cos, sin, qk_dim):
    n, d = h.shape
    n_out = w.shape[2]
    hd = cos.shape[1]
    pos_rows = cos.shape[0]
    tm = _tile(math.gcd(rows_per_mod, pos_rows), 1024)
    tn = _tile(math.gcd(qk_dim, n_out), 1024, hd)
    n_tiles = n // tm
    pos_tiles = pos_rows // tm
    kern = functools.partial(_proj_odd_kernel, n_steps=n_out // tn, q_tiles=qk_dim // tn,
                             qk_tiles=2 * qk_dim // tn, hd=hd, k_scale=hd ** -0.5)
    return pl.pallas_call(
        kern,
        grid=(n_tiles, n_out // tn),
        in_specs=_first_tile_specs(tm, d, shift, scale) + [
            _row_spec(tm, d, n_tiles),
            _mod_spec(shift, rows_per_mod, tm, n_tiles), _mod_spec(scale, rows_per_mod, tm, n_tiles),
            pl.BlockSpec((None, d, tn), lambda i, j: (layer, 0, j)),
            pl.BlockSpec((tm, hd), lambda i, j: (i % pos_tiles, 0)),
            pl.BlockSpec((tm, hd), lambda i, j: (i % pos_tiles, 0)),
        ],
        out_specs=pl.BlockSpec((tm, tn), lambda i, j: (i, j)),
        out_shape=jax.ShapeDtypeStruct((n, n_out), BF16),
        scratch_shapes=[pltpu.VMEM((2, tm, d), BF16)],
        compiler_params=_params("arbitrary", "arbitrary"),
        name="proj_odd",
    )(h, shift, scale, h, shift, scale, w, cos, sin)


def _retention_kernel(lgf_ref, lgb_ref, qc_ref, kc_ref, vc_ref, gc_ref, ql_ref, kl_ref, vl_ref, gl_ref,
                      oc_ref, ol_ref, sf_scr, sb_scr, d_scr, dec_scr, partc_scr, partl_scr, *, chunk):
    c = chunk
    dk = sf_scr.shape[0]
    lgf = lgf_ref[...][:, :1]
    lgb = lgb_ref[...][:, :1]
    diff = (lax.broadcasted_iota(jnp.int32, (c, c), 0)
            - lax.broadcasted_iota(jnp.int32, (c, c), 1)).astype(F32)
    d_scr[...] = jnp.where(diff >= 0.0, jnp.exp(jnp.maximum(diff, 0.0) * lgf),
                           jnp.exp(jnp.maximum(-diff, 0.0) * lgb))
    idx = lax.broadcasted_iota(jnp.int32, (c, dk), 0).astype(F32)
    dec_scr[0] = jnp.exp((idx + 1.0) * lgf).astype(BF16)
    dec_scr[1] = jnp.exp((c - 1.0 - idx) * lgf).astype(BF16)
    dec_scr[2] = jnp.exp((c - idx) * lgb).astype(BF16)
    dec_scr[3] = jnp.exp(idx * lgb).astype(BF16)
    chunk_dec_f, chunk_dec_b = jnp.exp(c * lgf), jnp.exp(c * lgb)
    nt = (((1,), (1,)), ((), ()))
    tn = (((0,), (0,)), ((), ()))

    sf_scr[...] = jnp.zeros_like(sf_scr)
    sb_scr[...] = jnp.zeros_like(sb_scr)

    def forward(seg, i):
        q_ref, k_ref, v_ref = seg[:3]
        rows = slice(i * c, (i + 1) * c)
        q, k, v = q_ref[rows, :], k_ref[rows, :], v_ref[rows, :]
        s = lax.dot_general(q, k, nt, preferred_element_type=F32) * d_scr[...]
        lhs = jnp.concatenate([s.astype(BF16), q * dec_scr[0]], axis=1)
        rhs = jnp.concatenate([v, sf_scr[...].astype(BF16)], axis=0)
        o = jnp.dot(lhs, rhs, preferred_element_type=F32)
        sf_scr[...] = sf_scr[...] * chunk_dec_f + lax.dot_general(k * dec_scr[1], v, tn,
                                                                  preferred_element_type=F32)
        return o

    def backward(seg, i):
        q_ref, k_ref, v_ref = seg[:3]
        rows = slice(i * c, (i + 1) * c)
        q, k, v = q_ref[rows, :], k_ref[rows, :], v_ref[rows, :]
        o = jnp.dot(q * dec_scr[2], sb_scr[...].astype(BF16), preferred_element_type=F32)
        sb_scr[...] = sb_scr[...] * chunk_dec_b + lax.dot_general(k * dec_scr[3], v, tn,
                                                                  preferred_element_type=F32)
        return o

    def finish(seg, i, o):
        g_ref, o_ref = seg[3], seg[4]
        rows = slice(i * c, (i + 1) * c)
        o = o * lax.rsqrt(jnp.mean(o * o, axis=-1, keepdims=True) + EPS)
        o_ref[rows, :] = _silu(g_ref[rows, :]) * o.astype(BF16)

    def sweep(seg):
        part = seg[5]
        n = seg[0].shape[0] // c
        for t in range(n):
            i_f, i_b = t, n - 1 - t
            o_f, o_b = forward(seg, i_f), backward(seg, i_b)
            if i_f == i_b:
                finish(seg, i_f, o_f + o_b)
            elif i_f < i_b:
                part[i_f * c:(i_f + 1) * c, :] = o_f
                part[i_b * c:(i_b + 1) * c, :] = o_b
            else:
                finish(seg, i_f, o_f + part[i_f * c:(i_f + 1) * c, :])
                finish(seg, i_b, o_b + part[i_b * c:(i_b + 1) * c, :])

    sweep((qc_ref, kc_ref, vc_ref, gc_ref, oc_ref, partc_scr))
    sweep((ql_ref, kl_ref, vl_ref, gl_ref, ol_ref, partl_scr))


def _retention(p_ctx, p_lat, lg_fwd, lg_bwd, batch):
    heads = lg_fwd.shape[0]
    n_out = p_lat.shape[1]
    dk = n_out // (6 * heads)
    dv = 2 * dk
    t_c, t_l = p_ctx.shape[0] // batch, p_lat.shape[0] // batch
    c = _tile(math.gcd(t_c, t_l), 256)
    lanes = lambda lg: jnp.broadcast_to(lg.astype(F32)[:, None, None], (heads, 1, V7X_LANES))
    lg_spec = pl.BlockSpec((None, 1, V7X_LANES), lambda b, h: (h, 0, 0))

    def specs(t):
        return [pl.BlockSpec((t, dk), lambda b, h: (b, h)),
                pl.BlockSpec((t, dk), lambda b, h: (b, heads + h)),
                pl.BlockSpec((t, dv), lambda b, h: (b, heads + h)),
                pl.BlockSpec((t, dv), lambda b, h: (b, 2 * heads + h))]

    out_spec = lambda t: pl.BlockSpec((t, dv), lambda b, h: (b, h))
    return pl.pallas_call(
        functools.partial(_retention_kernel, chunk=c),
        grid=(batch, heads),
        in_specs=[lg_spec, lg_spec] + specs(t_c) + specs(t_l),
        out_specs=[out_spec(t_c), out_spec(t_l)],
        out_shape=[jax.ShapeDtypeStruct((p_ctx.shape[0], heads * dv), BF16),
                   jax.ShapeDtypeStruct((p_lat.shape[0], heads * dv), BF16)],
        scratch_shapes=[pltpu.VMEM((dk, dv), F32), pltpu.VMEM((dk, dv), F32), pltpu.VMEM((c, c), F32),
                        pltpu.VMEM((4, c, dk), BF16),
                        pltpu.VMEM((t_c, dv), F32), pltpu.VMEM((t_l, dv), F32)],
        compiler_params=_params("parallel", "parallel"),
        name="retention",
    )(lanes(lg_fwd), lanes(lg_bwd), p_ctx, p_ctx, p_ctx, p_ctx, p_lat, p_lat, p_lat, p_lat)


def _outproj_kernel(*refs, n_parts):
    a_refs, w_refs = refs[:n_parts], refs[n_parts:2 * n_parts]
    res_ref, gate_ref, o_ref = refs[2 * n_parts:]
    acc = functools.reduce(jnp.add, [jnp.dot(a[...], w[...], preferred_element_type=F32)
                                     for a, w in zip(a_refs, w_refs)])
    o_ref[...] = res_ref[...] + gate_ref[...] * acc


def _outproj(parts, weights, layer, res, gate, rows_per_mod):
    n, d = res.shape
    tm = _tile(rows_per_mod, 1024)
    tn = _tile(d, 512, V7X_LANES)
    tiles = rows_per_mod // tm
    return pl.pallas_call(
        functools.partial(_outproj_kernel, n_parts=len(parts)),
        grid=(n // tm, d // tn),
        in_specs=[pl.BlockSpec((tm, a.shape[1]), lambda i, j: (i, 0)) for a in parts]
        + [pl.BlockSpec((None, w.shape[1], tn), lambda i, j: (layer, 0, j)) for w in weights]
        + [pl.BlockSpec((tm, tn), lambda i, j: (i, j)),
           pl.BlockSpec((None, 1, tn), lambda i, j: (i // tiles, 0, j))],
        out_specs=pl.BlockSpec((tm, tn), lambda i, j: (i, j)),
        out_shape=jax.ShapeDtypeStruct((n, d), F32),
        compiler_params=_params("parallel", "parallel"),
        name="outproj",
    )(*parts, *weights, res, gate)


def _ffn_kernel(x_ref, sh_ref, sc_ref, gate_ref, xn_ref, shn_ref, scn_ref, wg_ref, wu_ref, wo_ref,
                o_ref, u_scr, acc_scr, *, n_steps):
    i, j = pl.program_id(0), pl.program_id(1)
    slot = i % 2

    @pl.when((i == 0) & (j == 0))
    def _():
        _stage_modulated(x_ref, sh_ref, sc_ref, u_scr, 0, 0, 1)

    @pl.when(j == 0)
    def _():
        acc_scr[...] = jnp.zeros_like(acc_scr)

    u = u_scr[slot]
    g = jnp.dot(u, wg_ref[...], preferred_element_type=F32)
    up = jnp.dot(u, wu_ref[...], preferred_element_type=F32)
    act = (_silu(g) * up).astype(BF16)
    acc_scr[...] += jnp.dot(act, wo_ref[...], preferred_element_type=F32)
    _stage_modulated(xn_ref, shn_ref, scn_ref, u_scr, 1 - slot, j, n_steps)

    @pl.when(j == n_steps - 1)
    def _():
        o_ref[...] = x_ref[...] + gate_ref[...] * acc_scr[...]


def _ffn(h, shift, scale, gate, rows_per_mod, w_in, w_out, layer):
    n, d = h.shape
    f = w_out.shape[1]
    tm = _tile(rows_per_mod, 512)
    tf = _tile(f, 512, V7X_LANES)
    nf = f // tf
    n_tiles = n // tm
    return pl.pallas_call(
        functools.partial(_ffn_kernel, n_steps=nf),
        grid=(n_tiles, nf),
        in_specs=[
            _row_spec(tm, d),
            _mod_spec(shift, rows_per_mod, tm), _mod_spec(scale, rows_per_mod, tm),
            _mod_spec(gate, rows_per_mod, tm),
            _row_spec(tm, d, n_tiles),
            _mod_spec(shift, rows_per_mod, tm, n_tiles), _mod_spec(scale, rows_per_mod, tm, n_tiles),
            pl.BlockSpec((None, d, tf), lambda i, j: (layer, 0, j)),
            pl.BlockSpec((None, d, tf), lambda i, j: (layer, 0, nf + j)),
            pl.BlockSpec((None, tf, d), lambda i, j: (layer, j, 0)),
        ],
        out_specs=_row_spec(tm, d),
        out_shape=jax.ShapeDtypeStruct((n, d), F32),
        scratch_shapes=[pltpu.VMEM((2, tm, d), BF16), pltpu.VMEM((tm, d), F32)],
        compiler_params=_params("arbitrary", "arbitrary"),
        name="ffn",
    )(h, shift, scale, gate, h, shift, scale, w_in, w_in, w_out)


def _even_layer(h_ctx, h_lat, mod_ctx, mod_lat, weights, layer, q_gain, k_gain, batch, need_ctx, tables):
    n_ctx = h_ctx.shape[0]
    n_lat = h_lat.shape[0]
    hd = ATTN_HEAD_DIM
    w_in, w_f, w_a = weights
    fdim = w_f.shape[1]
    kvdim = ATTN_KV_HEADS * hd
    qdim = w_in.shape[2] - fdim - 2 * kvdim
    dims = (fdim, qdim, kvdim)
    q_gain, k_gain = q_gain.reshape(1, hd), k_gain.reshape(1, hd)

    f_c, q_c, k_c, v_c = _proj_even(h_ctx, mod_ctx[0], mod_ctx[1], n_ctx, w_in, layer, q_gain, k_gain,
                                    *tables["rope_even_ctx"], dims)
    f_l, q_l, k_l, v_l = _proj_even(h_lat, mod_lat[0], mod_lat[1], n_lat // batch, w_in, layer, q_gain, k_gain,
                                    *tables["rope_even_lat"], dims)
    a_l = _attention(q_l, [(k_l, v_l), (k_c, v_c)], batch)
    m_l = _fourier(f_l, batch, tables["dft_lat"])
    h_lat = _outproj([m_l, a_l], [w_f, w_a], layer, h_lat, mod_lat[2], n_lat // batch)
    if need_ctx:
        a_c = _attention(q_c, [(k_c, v_c)], batch)
        m_c = _fourier(f_c, batch, tables["dft_ctx"])
        h_ctx = _outproj([m_c, a_c], [w_f, w_a], layer, h_ctx, mod_ctx[2], n_ctx)
    return h_ctx, h_lat


def _odd_layer(h_ctx, h_lat, mod_ctx, mod_lat, weights, layer, lg_fwd, lg_bwd, batch, need_ctx, tables):
    n_ctx = h_ctx.shape[0]
    n_lat = h_lat.shape[0]
    w_in, w_out = weights
    qk_dim = w_in.shape[2] // 6
    p_c = _proj_odd(h_ctx, mod_ctx[0], mod_ctx[1], n_ctx, w_in, layer, *tables["rope_odd_ctx"], qk_dim)
    p_l = _proj_odd(h_lat, mod_lat[0], mod_lat[1], n_lat // batch, w_in, layer, *tables["rope_odd_lat"], qk_dim)
    m_c, m_l = _retention(p_c, p_l, lg_fwd, lg_bwd, batch)
    h_lat = _outproj([m_l], [w_out], layer, h_lat, mod_lat[2], n_lat // batch)
    if need_ctx:
        h_ctx = _outproj([m_c], [w_out], layer, h_ctx, mod_ctx[2], n_ctx)
    return h_ctx, h_lat


def kernel(x, c, ctx, c_ctx, w_mod, b_mod, w_in_even, w_out_even, q_gain_even, k_gain_even,
           w_in_odd, w_out_odd, log_decay_fwd, log_decay_bwd, w_ffn_in, w_ffn_out):
    batch, seq, d = x.shape
    ctx_len = ctx.shape[1]
    depth = w_mod.shape[0]
    n_lat, n_ctx = batch * seq, batch * ctx_len
    h_lat, h_ctx = x.reshape(n_lat, d), ctx.reshape(n_ctx, d)

    rows = -(-(batch + 1) // 8) * 8
    c_all = jnp.concatenate([c, c_ctx[None], jnp.zeros((rows - batch - 1, d), F32)], axis=0)
    mods = _modulation(c_all, w_mod, b_mod).reshape(depth, rows, N_MOD, d)

    fdim = FOURIER_GROUPS * (d // 16)
    even_w = (_to_bf16(w_in_even), _to_bf16(w_out_even, 0, fdim), _to_bf16(w_out_even, fdim))
    odd_w = (_to_bf16(w_in_odd), _to_bf16(w_out_odd))
    ffn_w = (_to_bf16(w_ffn_in), _to_bf16(w_ffn_out))

    ret_hd = w_in_odd.shape[2] // (6 * log_decay_fwd.shape[1])
    ctx_rows = _tile(n_ctx, 1024)
    tables = {
        "rope_even_lat": _rope_tables(seq, ATTN_HEAD_DIM, False),
        "rope_even_ctx": _rope_tables(ctx_rows, ATTN_HEAD_DIM, True),
        "rope_odd_lat": _rope_tables(seq, ret_hd, False),
        "rope_odd_ctx": _rope_tables(ctx_rows, ret_hd, True),
        "dft_lat": _dft_tables(seq, d // 16),
        "dft_ctx": _dft_tables(ctx_len, d // 16),
    }

    for i in range(depth):
        need_ctx = i < depth - 1
        mod_lat = [mods[i, :batch, k][:, None, :] for k in range(N_MOD)]
        mod_ctx = [mods[i, batch:batch + 1, k][:, None, :] for k in range(N_MOD)]
        j = i // 2
        if i % 2 == 0:
            h_ctx, h_lat = _even_layer(h_ctx, h_lat, mod_ctx, mod_lat, even_w, j,
                                       q_gain_even[j], k_gain_even[j], batch, need_ctx, tables)
        else:
            h_ctx, h_lat = _odd_layer(h_ctx, h_lat, mod_ctx, mod_lat, odd_w, j,
                                      log_decay_fwd[j], log_decay_bwd[j], batch, need_ctx, tables)
        h_lat = _ffn(h_lat, mod_lat[3], mod_lat[4], mod_lat[5], n_lat // batch, *ffn_w, i)
        if need_ctx:
            h_ctx = _ffn(h_ctx, mod_ctx[3], mod_ctx[4], mod_ctx[5], n_ctx, *ffn_w, i)
    return h_lat.reshape(batch, seq, d)
```

```python
import functools
import math

import jax
import jax.numpy as jnp
import numpy as np
from jax import lax
from jax.experimental import pallas as pl
from jax.experimental.pallas import tpu as pltpu

F32 = jnp.float32
BF16 = jnp.bfloat16

EPS = 1e-6
ROPE_BASE = 10000.0
GRID_W = 64
N_MOD = 6
FOURIER_GROUPS = 4
ATTN_HEAD_DIM = 128
ATTN_KV_HEADS = 4

V7X_LANES = 128
V7X_VMEM_BYTES = 64 * 1024 * 1024
VMEM_LIMIT_BYTES = V7X_VMEM_BYTES - 8 * 1024 * 1024


def _tile(n, target, align=8):
    best = None
    for t in range(align, min(n, target) + 1, align):
        if n % t == 0:
            best = t
    return n if best is None else best


def _params(*semantics):
    return pltpu.CompilerParams(dimension_semantics=semantics, vmem_limit_bytes=VMEM_LIMIT_BYTES)


def _silu(x):
    return x * jax.nn.sigmoid(x)


def _norm_mod(x, shift, scale):
    ms = jnp.mean(x * x, axis=-1, keepdims=True)
    return x * lax.rsqrt(ms + EPS) * (1.0 + scale) + shift


def _next_tile(i, n_tiles):
    return jnp.minimum(i + 1, n_tiles - 1)


def _row_spec(tm, width, n_tiles=None):
    if n_tiles is None:
        return pl.BlockSpec((tm, width), lambda i, *_: (i, 0))
    return pl.BlockSpec((tm, width), lambda i, *_: (_next_tile(i, n_tiles), 0))


def _mod_spec(mod, rows_per_mod, tm, n_tiles=None):
    tiles = rows_per_mod // tm
    d = mod.shape[-1]
    if n_tiles is None:
        return pl.BlockSpec((None, 1, d), lambda i, *_: (i // tiles, 0, 0))
    return pl.BlockSpec((None, 1, d), lambda i, *_: (_next_tile(i, n_tiles) // tiles, 0, 0))


ROW_CHUNK = 16
LANE_CHUNK = 1024


def _modulate_rows(x_ref, sh_ref, sc_ref, dst, start=0, n_rows=None):
    n_rows = x_ref.shape[0] if n_rows is None else n_rows
    d = x_ref.shape[1]
    for r0 in range(0, n_rows, ROW_CHUNK):
        first = start + r0
        rows = pl.ds(first if isinstance(first, int) else pl.multiple_of(first, ROW_CHUNK), ROW_CHUNK)
        x = x_ref[rows, :]
        inv = lax.rsqrt(jnp.mean(x * x, axis=-1, keepdims=True) + EPS)
        for c0 in range(0, d, LANE_CHUNK):
            cols = slice(c0, min(c0 + LANE_CHUNK, d))
            y = x_ref[rows, cols] * inv * (1.0 + sc_ref[:, cols]) + sh_ref[:, cols]
            dst[rows, cols] = y.astype(BF16)


def _stage_modulated(x_ref, sh_ref, sc_ref, u_scr, slot, step, n_steps):
    tm = x_ref.shape[0]
    if n_steps == 1:
        _modulate_rows(x_ref, sh_ref, sc_ref, u_scr.at[slot])
    else:
        r = min(tm, -(-(-(-tm // n_steps)) // ROW_CHUNK) * ROW_CHUNK)
        start = pl.multiple_of(jnp.minimum(step * r, tm - r), ROW_CHUNK)
        _modulate_rows(x_ref, sh_ref, sc_ref, u_scr.at[slot], start, r)


def _mod_kernel(c_ref, w_ref, b_ref, o_ref):
    cond = _silu(c_ref[...]).astype(BF16)
    o_ref[...] = jnp.dot(cond, w_ref[...].astype(BF16), preferred_element_type=F32) + b_ref[...]


def _modulation(c_all, w_mod, b_mod):
    depth, d, n = w_mod.shape
    r = c_all.shape[0]
    tn = _tile(n, 1024, V7X_LANES)
    return pl.pallas_call(
        _mod_kernel,
        grid=(depth, n // tn),
        in_specs=[
            pl.BlockSpec((r, d), lambda i, j: (0, 0)),
            pl.BlockSpec((None, d, tn), lambda i, j: (i, 0, j)),
            pl.BlockSpec((None, 1, tn), lambda i, j: (i, 0, j)),
        ],
        out_specs=pl.BlockSpec((None, r, tn), lambda i, j: (i, 0, j)),
        out_shape=jax.ShapeDtypeStruct((depth, r, n), F32),
        compiler_params=_params("parallel", "parallel"),
        name="modulation",
    )(c_all, w_mod, b_mod.reshape(depth, 1, n))


def _cast_kernel(x_ref, o_ref):
    o_ref[...] = x_ref[...].astype(BF16)


def _to_bf16(w, row_lo=0, row_hi=None):
    layers, k, n = w.shape
    row_hi = k if row_hi is None else row_hi
    rows = row_hi - row_lo
    tk = _tile(math.gcd(rows, row_lo), 256)
    tn = _tile(n, 4096, V7X_LANES)
    first = row_lo // tk
    return pl.pallas_call(
        _cast_kernel,
        grid=(layers, rows // tk, n // tn),
        in_specs=[pl.BlockSpec((None, tk, tn), lambda l, r, c: (l, first + r, c))],
        out_specs=pl.BlockSpec((None, tk, tn), lambda l, r, c: (l, r, c)),
        out_shape=jax.ShapeDtypeStruct((layers, rows, n), BF16),
        compiler_params=_params("parallel", "parallel", "parallel"),
        name="to_bf16",
    )(w)


def _rope_angles(n_tokens, dp):
    t = np.arange(n_tokens)
    inv = ROPE_BASE ** (-np.arange(0, dp, 2, dtype=np.float64) / dp)
    return (t // GRID_W)[:, None] * inv[None, :], (t % GRID_W)[:, None] * inv[None, :]


def _rope_tables(n_tokens, head_dim, identity):
    half = head_dim // 2
    if identity:
        return jnp.ones((n_tokens, head_dim), F32), jnp.zeros((n_tokens, head_dim), F32)
    row, col = _rope_angles(n_tokens, half)
    cos = np.concatenate([np.cos(row), np.cos(row), np.cos(col), np.cos(col)], axis=-1)
    sin = np.concatenate([-np.sin(row), np.sin(row), -np.sin(col), np.sin(col)], axis=-1)
    return jnp.asarray(cos, F32), jnp.asarray(sin, F32)


def _rope_apply(y, cos, sin, quarter):
    if 2 * quarter == V7X_LANES:
        return y * cos + pltpu.roll(y, quarter, 1) * sin
    lane = lax.broadcasted_iota(jnp.int32, y.shape, 1)
    first = (lane % (2 * quarter)) < quarter
    partner = jnp.where(first, pltpu.roll(y, V7X_LANES - quarter, 1), pltpu.roll(y, quarter, 1))
    return y * cos + partner * sin


def _proj_even_kernel(x_ref, sh_ref, sc_ref, xn_ref, shn_ref, scn_ref, w_ref, qg_ref, kg_ref, cos_ref, sin_ref,
                      f_ref, q_ref, k_ref, v_ref, u_scr, *, fdim, qdim, kvdim, hd):
    i = pl.program_id(0)
    slot = i % 2

    @pl.when(i == 0)
    def _():
        _stage_modulated(x_ref, sh_ref, sc_ref, u_scr, 0, 0, 1)

    u = u_scr[slot]
    cos = cos_ref[...]
    sin = sin_ref[...]

    def heads(lo, width, gain_ref, out_ref):
        seg = _tile(width, 512, hd)
        gain = gain_ref[...]
        for s in range(width // seg):
            y = jnp.dot(u, w_ref[:, lo + s * seg:lo + (s + 1) * seg], preferred_element_type=F32)
            for h in range(seg // hd):
                yh = y[:, h * hd:(h + 1) * hd]
                ms = jnp.mean(yh * yh, axis=-1, keepdims=True)
                yh = yh * lax.rsqrt(ms + EPS) * gain
                yh = _rope_apply(yh, cos, sin, hd // 4)
                out_ref[:, s * seg + h * hd:s * seg + (h + 1) * hd] = yh.astype(BF16)

    heads(fdim, qdim, qg_ref, q_ref)
    heads(fdim + qdim, kvdim, kg_ref, k_ref)
    f_ref[...] = jnp.dot(u, w_ref[:, :fdim], preferred_element_type=F32).astype(BF16)
    v_ref[...] = jnp.dot(u, w_ref[:, fdim + qdim + kvdim:], preferred_element_type=F32).astype(BF16)
    _stage_modulated(xn_ref, shn_ref, scn_ref, u_scr, 1 - slot, 0, 1)


def _first_tile_specs(tm, d, shift, scale):
    once = pl.Buffered(1)
    return [pl.BlockSpec((tm, d), lambda i, *_: (0, 0), pipeline_mode=once),
            pl.BlockSpec((None, 1, d), lambda i, *_: (0, 0, 0), pipeline_mode=once),
            pl.BlockSpec((None, 1, d), lambda i, *_: (0, 0, 0), pipeline_mode=once)]


def _proj_even(h, shift, scale, rows_per_mod, w, layer, q_gain, k_gain, cos, sin, dims):
    n, d = h.shape
    fdim, qdim, kvdim = dims
    hd = ATTN_HEAD_DIM
    pos_rows = cos.shape[0]
    tm = _tile(math.gcd(rows_per_mod, pos_rows), 512)
    n_tiles = n // tm
    pos_tiles = pos_rows // tm
    kern = functools.partial(_proj_even_kernel, fdim=fdim, qdim=qdim, kvdim=kvdim, hd=hd)
    const = lambda a: pl.BlockSpec(a.shape, lambda i: (0,) * a.ndim)
    return pl.pallas_call(
        kern,
        grid=(n_tiles,),
        in_specs=_first_tile_specs(tm, d, shift, scale) + [
            _row_spec(tm, d, n_tiles),
            _mod_spec(shift, rows_per_mod, tm, n_tiles), _mod_spec(scale, rows_per_mod, tm, n_tiles),
            pl.BlockSpec((None,) + w.shape[1:], lambda i: (layer, 0, 0)),
            const(q_gain), const(k_gain),
            pl.BlockSpec((tm, hd), lambda i: (i % pos_tiles, 0)),
            pl.BlockSpec((tm, hd), lambda i: (i % pos_tiles, 0)),
        ],
        out_specs=[_row_spec(tm, fdim), _row_spec(tm, qdim), _row_spec(tm, kvdim), _row_spec(tm, kvdim)],
        out_shape=[jax.ShapeDtypeStruct((n, fdim), BF16), jax.ShapeDtypeStruct((n, qdim), BF16),
                   jax.ShapeDtypeStruct((n, kvdim), BF16), jax.ShapeDtypeStruct((n, kvdim), BF16)],
        scratch_shapes=[pltpu.VMEM((2, tm, d), BF16)],
        compiler_params=_params("arbitrary"),
        name="proj_even",
    )(h, shift, scale, h, shift, scale, w, q_gain, k_gain, cos, sin)


def _attn_kernel(q_ref, *refs, n_seg, group, hd, scale, n_split):
    k_refs, v_refs = refs[:n_seg], refs[n_seg:2 * n_seg]
    o_ref, s_scr, p_scr, v_scr = refs[2 * n_seg:]
    tq = q_ref.shape[0]

    @pl.when(pl.program_id(2) == 0)
    def _():
        off = 0
        for v in v_refs:
            n = v.shape[0]
            v_scr[off:off + n, :hd] = v[...]
            v_scr[off:off + n, hd:] = jnp.ones((n, hd), BF16)
            off += n

    q = q_ref[...]
    qs = jnp.concatenate([q[:, g * hd:(g + 1) * hd] for g in range(group)], axis=0)
    nt = (((1,), (1,)), ((), ()))
    rows = qs.shape[0] // n_split
    blocks = [slice(i * rows, (i + 1) * rows) for i in range(n_split)]
    for blk in blocks:
        off = 0
        for k in k_refs:
            n = k.shape[0]
            s_scr[blk, off:off + n] = lax.dot_general(qs[blk], k[...], nt, preferred_element_type=F32)
            off += n
    outs = []
    for blk in blocks:
        m = jnp.max(s_scr[blk, :], axis=-1, keepdims=True)
        p_scr[blk, :] = jnp.exp2((s_scr[blk, :] - m) * (scale * math.log2(math.e))).astype(BF16)
    for blk in blocks:
        o = jnp.dot(p_scr[blk, :], v_scr[...], preferred_element_type=F32)
        outs.append(o[:, :hd] / o[:, hd:])
    o = jnp.concatenate(outs, axis=0)
    for g in range(group):
        o_ref[:, g * hd:(g + 1) * hd] = o[g * tq:(g + 1) * tq].astype(BF16)


def _attention(q, segments, batch):
    hd, kvh = ATTN_HEAD_DIM, ATTN_KV_HEADS
    n, qdim = q.shape
    group = qdim // (kvh * hd)
    t_q = n // batch
    tq = _tile(t_q, 512)
    nq = t_q // tq
    n_keys = sum(k.shape[0] for k, _ in segments) // batch
    kern = functools.partial(_attn_kernel, n_seg=len(segments), group=group, hd=hd, scale=hd ** -0.5,
                             n_split=4)
    kv_spec = lambda a: pl.BlockSpec((a.shape[0] // batch, hd), lambda b, kh, t: (b, kh))
    q_spec = pl.BlockSpec((tq, group * hd), lambda b, kh, t: (b * nq + t, kh))
    return pl.pallas_call(
        kern,
        grid=(batch, kvh, nq),
        in_specs=[q_spec] + [kv_spec(k) for k, _ in segments] + [kv_spec(v) for _, v in segments],
        out_specs=q_spec,
        out_shape=jax.ShapeDtypeStruct((n, qdim), BF16),
        scratch_shapes=[pltpu.VMEM((group * tq, n_keys), F32), pltpu.VMEM((group * tq, n_keys), BF16),
                        pltpu.VMEM((n_keys, 2 * hd), BF16)],
        compiler_params=_params("parallel", "parallel", "arbitrary"),
        name="attention",
    )(q, *[k for k, _ in segments], *[v for _, v in segments])


def _fourier_kernel(u_ref, cm_ref, cs_ref, o_ref, z_scr, *, groups, gd, scale):
    t = u_ref.shape[0]

    @pl.when(pl.program_id(1) == 0)
    def _():
        for g in range(groups):
            cols = slice(g * gd, (g + 1) * gd)
            z = jnp.dot(u_ref[:, cols], cm_ref[...], preferred_element_type=F32)
            z_scr[0:t, cols] = z[:, :gd].astype(BF16)
            z_scr[t:2 * t, cols] = z[:, gd:].astype(BF16)

    y = jnp.dot(cs_ref[...], z_scr[...], preferred_element_type=F32)
    o_ref[...] = (y * scale).astype(BF16)


def _dft_tables(t, gd):
    def angles(n):
        i = np.arange(n)
        return ((i[:, None] * i[None, :]) % n) * (2.0 * math.pi / n)
    a_c, a_t = angles(gd), angles(t)
    cm = np.concatenate([np.cos(a_c), -np.sin(a_c)], axis=1)
    cs = np.concatenate([np.cos(a_t), np.sin(a_t)], axis=1)
    return jnp.asarray(cm.astype(BF16)), jnp.asarray(cs.astype(BF16))


def _fourier(u, batch, tables):
    n, fdim = u.shape
    t = n // batch
    gd = fdim // FOURIER_GROUPS
    cm, cs = tables
    tr = _tile(t, 512)
    kern = functools.partial(_fourier_kernel, groups=FOURIER_GROUPS, gd=gd, scale=(t * gd) ** -0.5)
    return pl.pallas_call(
        kern,
        grid=(batch, t // tr),
        in_specs=[
            pl.BlockSpec((t, fdim), lambda b, r: (b, 0)),
            pl.BlockSpec(cm.shape, lambda b, r: (0, 0)),
            pl.BlockSpec((tr, 2 * t), lambda b, r: (r, 0)),
        ],
        out_specs=pl.BlockSpec((tr, fdim), lambda b, r: (b * (t // tr) + r, 0)),
        out_shape=jax.ShapeDtypeStruct((n, fdim), BF16),
        scratch_shapes=[pltpu.VMEM((2 * t, fdim), BF16)],
        compiler_params=_params("parallel", "arbitrary"),
        name="fourier",
    )(u, cm, cs)


def _proj_odd_kernel(x_ref, sh_ref, sc_ref, xn_ref, shn_ref, scn_ref, w_ref, cos_ref, sin_ref, o_ref, u_scr,
                     *, n_steps, q_tiles, qk_tiles, hd, k_scale):
    i, j = pl.program_id(0), pl.program_id(1)
    slot = i % 2

    @pl.when((i == 0) & (j == 0))
    def _():
        _stage_modulated(x_ref, sh_ref, sc_ref, u_scr, 0, 0, 1)

    y = jnp.dot(u_scr[slot], w_ref[...], preferred_element_type=F32)
    tn = y.shape[1]
    _stage_modulated(xn_ref, shn_ref, scn_ref, u_scr, 1 - slot, j, n_steps)

    @pl.when(j < qk_tiles)
    def _():
        mult = jnp.where(j < q_tiles, 1.0, k_scale).astype(F32)
        for c in range(tn // V7X_LANES):
            lanes = slice(c * V7X_LANES, (c + 1) * V7X_LANES)
            tab = slice((c % (hd // V7X_LANES)) * V7X_LANES, (c % (hd // V7X_LANES) + 1) * V7X_LANES)
            r = _rope_apply(y[:, lanes], cos_ref[:, tab], sin_ref[:, tab], hd // 4)
            o_ref[:, lanes] = (r * mult).astype(BF16)

    @pl.when(j >= qk_tiles)
    def _():
        o_ref[...] = y.astype(BF16)


def _proj_odd(h, shift, scale, rows_per_mod, w, layer, cos, sin, qk_dim):
    n, d = h.shape
    n_out = w.shape[2]
    hd = cos.shape[1]
    pos_rows = cos.shape[0]
    tm = _tile(math.gcd(rows_per_mod, pos_rows), 1024)
    tn = _tile(math.gcd(qk_dim, n_out), 1024, hd)
    n_tiles = n // tm
    pos_tiles = pos_rows // tm
    kern = functools.partial(_proj_odd_kernel, n_steps=n_out // tn, q_tiles=qk_dim // tn,
                             qk_tiles=2 * qk_dim // tn, hd=hd, k_scale=hd ** -0.5)
    return pl.pallas_call(
        kern,
        grid=(n_tiles, n_out // tn),
        in_specs=_first_tile_specs(tm, d, shift, scale) + [
            _row_spec(tm, d, n_tiles),
            _mod_spec(shift, rows_per_mod, tm, n_tiles), _mod_spec(scale, rows_per_mod, tm, n_tiles),
            pl.BlockSpec((None, d, tn), lambda i, j: (layer, 0, j)),
            pl.BlockSpec((tm, hd), lambda i, j: (i % pos_tiles, 0)),
            pl.BlockSpec((tm, hd), lambda i, j: (i % pos_tiles, 0)),
        ],
        out_specs=pl.BlockSpec((tm, tn), lambda i, j: (i, j)),
        out_shape=jax.ShapeDtypeStruct((n, n_out), BF16),
        scratch_shapes=[pltpu.VMEM((2, tm, d), BF16)],
        compiler_params=_params("arbitrary", "arbitrary"),
        name="proj_odd",
    )(h, shift, scale, h, shift, scale, w, cos, sin)


def _retention_kernel(lgf_ref, lgb_ref, qc_ref, kc_ref, vc_ref, gc_ref, ql_ref, kl_ref, vl_ref, gl_ref,
                      oc_ref, ol_ref, sf_scr, sb_scr, d_scr, dec_scr, partc_scr, partl_scr, *, chunk):
    c = chunk
    dk = sf_scr.shape[0]
    lgf = lgf_ref[...][:, :1]
    lgb = lgb_ref[...][:, :1]
    diff = (lax.broadcasted_iota(jnp.int32, (c, c), 0)
            - lax.broadcasted_iota(jnp.int32, (c, c), 1)).astype(F32)
    d_scr[...] = jnp.where(diff >= 0.0, jnp.exp(jnp.maximum(diff, 0.0) * lgf),
                           jnp.exp(jnp.maximum(-diff, 0.0) * lgb))
    idx = lax.broadcasted_iota(jnp.int32, (c, dk), 0).astype(F32)
    dec_scr[0] = jnp.exp((idx + 1.0) * lgf).astype(BF16)
    dec_scr[1] = jnp.exp((c - 1.0 - idx) * lgf).astype(BF16)
    dec_scr[2] = jnp.exp((c - idx) * lgb).astype(BF16)
    dec_scr[3] = jnp.exp(idx * lgb).astype(BF16)
    chunk_dec_f, chunk_dec_b = jnp.exp(c * lgf), jnp.exp(c * lgb)
    nt = (((1,), (1,)), ((), ()))
    tn = (((0,), (0,)), ((), ()))

    sf_scr[...] = jnp.zeros_like(sf_scr)
    sb_scr[...] = jnp.zeros_like(sb_scr)

    def forward(seg, i):
        q_ref, k_ref, v_ref = seg[:3]
        rows = slice(i * c, (i + 1) * c)
        q, k, v = q_ref[rows, :], k_ref[rows, :], v_ref[rows, :]
        s = lax.dot_general(q, k, nt, preferred_element_type=F32) * d_scr[...]
        lhs = jnp.concatenate([s.astype(BF16), q * dec_scr[0]], axis=1)
        rhs = jnp.concatenate([v, sf_scr[...].astype(BF16)], axis=0)
        o = jnp.dot(lhs, rhs, preferred_element_type=F32)
        sf_scr[...] = sf_scr[...] * chunk_dec_f + lax.dot_general(k * dec_scr[1], v, tn,
                                                                  preferred_element_type=F32)
        return o

    def backward(seg, i):
        q_ref, k_ref, v_ref = seg[:3]
        rows = slice(i * c, (i + 1) * c)
        q, k, v = q_ref[rows, :], k_ref[rows, :], v_ref[rows, :]
        o = jnp.dot(q * dec_scr[2], sb_scr[...].astype(BF16), preferred_element_type=F32)
        sb_scr[...] = sb_scr[...] * chunk_dec_b + lax.dot_general(k * dec_scr[3], v, tn,
                                                                  preferred_element_type=F32)
        return o

    def finish(seg, i, o):
        g_ref, o_ref = seg[3], seg[4]
        rows = slice(i * c, (i + 1) * c)
        o = o * lax.rsqrt(jnp.mean(o * o, axis=-1, keepdims=True) + EPS)
        o_ref[rows, :] = _silu(g_ref[rows, :]) * o.astype(BF16)

    def sweep(seg):
        part = seg[5]
        n = seg[0].shape[0] // c
        for t in range(n):
            i_f, i_b = t, n - 1 - t
            o_f, o_b = forward(seg, i_f), backward(seg, i_b)
            if i_f == i_b:
                finish(seg, i_f, o_f + o_b)
            elif i_f < i_b:
                part[i_f * c:(i_f + 1) * c, :] = o_f
                part[i_b * c:(i_b + 1) * c, :] = o_b
            else:
                finish(seg, i_f, o_f + part[i_f * c:(i_f + 1) * c, :])
                finish(seg, i_b, o_b + part[i_b * c:(i_b + 1) * c, :])

    sweep((qc_ref, kc_ref, vc_ref, gc_ref, oc_ref, partc_scr))
    sweep((ql_ref, kl_ref, vl_ref, gl_ref, ol_ref, partl_scr))


def _retention(p_ctx, p_lat, lg_fwd, lg_bwd, batch):
    heads = lg_fwd.shape[0]
    n_out = p_lat.shape[1]
    dk = n_out // (6 * heads)
    dv = 2 * dk
    t_c, t_l = p_ctx.shape[0] // batch, p_lat.shape[0] // batch
    c = _tile(math.gcd(t_c, t_l), 256)
    lanes = lambda lg: jnp.broadcast_to(lg.astype(F32)[:, None, None], (heads, 1, V7X_LANES))
    lg_spec = pl.BlockSpec((None, 1, V7X_LANES), lambda b, h: (h, 0, 0))

    def specs(t):
        return [pl.BlockSpec((t, dk), lambda b, h: (b, h)),
                pl.BlockSpec((t, dk), lambda b, h: (b, heads + h)),
                pl.BlockSpec((t, dv), lambda b, h: (b, heads + h)),
                pl.BlockSpec((t, dv), lambda b, h: (b, 2 * heads + h))]

    out_spec = lambda t: pl.BlockSpec((t, dv), lambda b, h: (b, h))
    return pl.pallas_call(
        functools.partial(_retention_kernel, chunk=c),
        grid=(batch, heads),
        in_specs=[lg_spec, lg_spec] + specs(t_c) + specs(t_l),
        out_specs=[out_spec(t_c), out_spec(t_l)],
        out_shape=[jax.ShapeDtypeStruct((p_ctx.shape[0], heads * dv), BF16),
                   jax.ShapeDtypeStruct((p_lat.shape[0], heads * dv), BF16)],
        scratch_shapes=[pltpu.VMEM((dk, dv), F32), pltpu.VMEM((dk, dv), F32), pltpu.VMEM((c, c), F32),
                        pltpu.VMEM((4, c, dk), BF16),
                        pltpu.VMEM((t_c, dv), F32), pltpu.VMEM((t_l, dv), F32)],
        compiler_params=_params("parallel", "parallel"),
        name="retention",
    )(lanes(lg_fwd), lanes(lg_bwd), p_ctx, p_ctx, p_ctx, p_ctx, p_lat, p_lat, p_lat, p_lat)


def _outproj_kernel(*refs, n_parts, seg):
    a_refs, w_refs = refs[:n_parts], refs[n_parts:2 * n_parts]
    res_ref, gate_ref, o_ref = refs[2 * n_parts:]
    for c0 in range(0, o_ref.shape[1], seg):
        cols = slice(c0, c0 + seg)
        acc = functools.reduce(jnp.add, [jnp.dot(a[...], w[:, cols], preferred_element_type=F32)
                                         for a, w in zip(a_refs, w_refs)])
        o_ref[:, cols] = res_ref[:, cols] + gate_ref[:, cols] * acc


def _outproj(parts, weights, layer, res, gate, rows_per_mod):
    n, d = res.shape
    tm = _tile(rows_per_mod, 512)
    return pl.pallas_call(
        functools.partial(_outproj_kernel, n_parts=len(parts), seg=_tile(d, 512, V7X_LANES)),
        grid=(n // tm,),
        in_specs=[_row_spec(tm, a.shape[1]) for a in parts]
        + [pl.BlockSpec((None, w.shape[1], d), lambda i: (layer, 0, 0), pipeline_mode=pl.Buffered(1))
           for w in weights]
        + [_row_spec(tm, d), _mod_spec(gate, rows_per_mod, tm)],
        out_specs=_row_spec(tm, d),
        out_shape=jax.ShapeDtypeStruct((n, d), F32),
        compiler_params=_params("parallel"),
        name="outproj",
    )(*parts, *weights, res, gate)


def _ffn_kernel(x_ref, sh_ref, sc_ref, gate_ref, wg_ref, wu_ref, wo_ref, o_ref, u_scr, *, n_steps):
    j = pl.program_id(1)

    @pl.when(j == 0)
    def _():
        _modulate_rows(x_ref, sh_ref, sc_ref, u_scr)
        o_ref[...] = jnp.zeros_like(o_ref)

    u = u_scr[...]
    g = jnp.dot(u, wg_ref[...], preferred_element_type=F32)
    up = jnp.dot(u, wu_ref[...], preferred_element_type=F32)
    act = (_silu(g) * up).astype(BF16)
    o_ref[...] += jnp.dot(act, wo_ref[...], preferred_element_type=F32)

    @pl.when(j == n_steps - 1)
    def _():
        o_ref[...] = x_ref[...] + gate_ref[...] * o_ref[...]


def _ffn(h, shift, scale, gate, rows_per_mod, w_in, w_out, layer):
    n, d = h.shape
    f = w_out.shape[1]
    tm = _tile(rows_per_mod, 1024)
    tf = _tile(f, 512, V7X_LANES)
    nf = f // tf
    n_tiles = n // tm
    return pl.pallas_call(
        functools.partial(_ffn_kernel, n_steps=nf),
        grid=(n_tiles, nf),
        in_specs=[
            _row_spec(tm, d),
            _mod_spec(shift, rows_per_mod, tm), _mod_spec(scale, rows_per_mod, tm),
            _mod_spec(gate, rows_per_mod, tm),
            pl.BlockSpec((None, d, tf), lambda i, j: (layer, 0, j)),
            pl.BlockSpec((None, d, tf), lambda i, j: (layer, 0, nf + j)),
            pl.BlockSpec((None, tf, d), lambda i, j: (layer, j, 0)),
        ],
        out_specs=_row_spec(tm, d),
        out_shape=jax.ShapeDtypeStruct((n, d), F32),
        scratch_shapes=[pltpu.VMEM((tm, d), BF16)],
        compiler_params=_params("parallel", "arbitrary"),
        name="ffn",
    )(h, shift, scale, gate, w_in, w_in, w_out)


def _even_layer(h_ctx, h_lat, mod_ctx, mod_lat, weights, layer, q_gain, k_gain, batch, need_ctx, tables):
    n_ctx = h_ctx.shape[0]
    n_lat = h_lat.shape[0]
    hd = ATTN_HEAD_DIM
    w_in, w_f, w_a = weights
    fdim = w_f.shape[1]
    kvdim = ATTN_KV_HEADS * hd
    qdim = w_in.shape[2] - fdim - 2 * kvdim
    dims = (fdim, qdim, kvdim)
    q_gain, k_gain = q_gain.reshape(1, hd), k_gain.reshape(1, hd)

    f_c, q_c, k_c, v_c = _proj_even(h_ctx, mod_ctx[0], mod_ctx[1], n_ctx, w_in, layer, q_gain, k_gain,
                                    *tables["rope_even_ctx"], dims)
    f_l, q_l, k_l, v_l = _proj_even(h_lat, mod_lat[0], mod_lat[1], n_lat // batch, w_in, layer, q_gain, k_gain,
                                    *tables["rope_even_lat"], dims)
    a_l = _attention(q_l, [(k_l, v_l), (k_c, v_c)], batch)
    m_l = _fourier(f_l, batch, tables["dft_lat"])
    h_lat = _outproj([m_l, a_l], [w_f, w_a], layer, h_lat, mod_lat[2], n_lat // batch)
    if need_ctx:
        a_c = _attention(q_c, [(k_c, v_c)], batch)
        m_c = _fourier(f_c, batch, tables["dft_ctx"])
        h_ctx = _outproj([m_c, a_c], [w_f, w_a], layer, h_ctx, mod_ctx[2], n_ctx)
    return h_ctx, h_lat


def _odd_layer(h_ctx, h_lat, mod_ctx, mod_lat, weights, layer, lg_fwd, lg_bwd, batch, need_ctx, tables):
    n_ctx = h_ctx.shape[0]
    n_lat = h_lat.shape[0]
    w_in, w_out = weights
    qk_dim = w_in.shape[2] // 6
    p_c = _proj_odd(h_ctx, mod_ctx[0], mod_ctx[1], n_ctx, w_in, layer, *tables["rope_odd_ctx"], qk_dim)
    p_l = _proj_odd(h_lat, mod_lat[0], mod_lat[1], n_lat // batch, w_in, layer, *tables["rope_odd_lat"], qk_dim)
    m_c, m_l = _retention(p_c, p_l, lg_fwd, lg_bwd, batch)
    h_lat = _outproj([m_l], [w_out], layer, h_lat, mod_lat[2], n_lat // batch)
    if need_ctx:
        h_ctx = _outproj([m_c], [w_out], layer, h_ctx, mod_ctx[2], n_ctx)
    return h_ctx, h_lat


def kernel(x, c, ctx, c_ctx, w_mod, b_mod, w_in_even, w_out_even, q_gain_even, k_gain_even,
           w_in_odd, w_out_odd, log_decay_fwd, log_decay_bwd, w_ffn_in, w_ffn_out):
    batch, seq, d = x.shape
    ctx_len = ctx.shape[1]
    depth = w_mod.shape[0]
    n_lat, n_ctx = batch * seq, batch * ctx_len
    h_lat, h_ctx = x.reshape(n_lat, d), ctx.reshape(n_ctx, d)

    rows = -(-(batch + 1) // 8) * 8
    c_all = jnp.concatenate([c, c_ctx[None], jnp.zeros((rows - batch - 1, d), F32)], axis=0)
    mods = _modulation(c_all, w_mod, b_mod).reshape(depth, rows, N_MOD, d)

    fdim = FOURIER_GROUPS * (d // 16)
    even_w = (_to_bf16(w_in_even), _to_bf16(w_out_even, 0, fdim), _to_bf16(w_out_even, fdim))
    odd_w = (_to_bf16(w_in_odd), _to_bf16(w_out_odd))
    ffn_w = (_to_bf16(w_ffn_in), _to_bf16(w_ffn_out))

    ret_hd = w_in_odd.shape[2] // (6 * log_decay_fwd.shape[1])
    ctx_rows = _tile(n_ctx, 1024)
    tables = {
        "rope_even_lat": _rope_tables(seq, ATTN_HEAD_DIM, False),
        "rope_even_ctx": _rope_tables(ctx_rows, ATTN_HEAD_DIM, True),
        "rope_odd_lat": _rope_tables(seq, ret_hd, False),
        "rope_odd_ctx": _rope_tables(ctx_rows, ret_hd, True),
        "dft_lat": _dft_tables(seq, d // 16),
        "dft_ctx": _dft_tables(ctx_len, d // 16),
    }

    for i in range(depth):
        need_ctx = i < depth - 1
        mod_lat = [mods[i, :batch, k][:, None, :] for k in range(N_MOD)]
        mod_ctx = [mods[i, batch:batch + 1, k][:, None, :] for k in range(N_MOD)]
        j = i // 2
        if i % 2 == 0:
            h_ctx, h_lat = _even_layer(h_ctx, h_lat, mod_ctx, mod_lat, even_w, j,
                                       q_gain_even[j], k_gain_even[j], batch, need_ctx, tables)
        else:
            h_ctx, h_lat = _odd_layer(h_ctx, h_lat, mod_ctx, mod_lat, odd_w, j,
                                      log_decay_fwd[j], log_decay_bwd[j], batch, need_ctx, tables)
        h_lat = _ffn(h_lat, mod_lat[3], mod_lat[4], mod_lat[5], n_lat // batch, *ffn_w, i)
        if need_ctx:
            h_ctx = _ffn(h_ctx, mod_ctx[3], mod_ctx[4], mod_ctx[5], n_ctx, *ffn_w, i)
    return h_lat.reshape(batch, seq, d)
```

```python
import functools
import math

import jax
import jax.numpy as jnp
import numpy as np
from jax import lax
from jax.experimental import pallas as pl
from jax.experimental.pallas import tpu as pltpu

F32 = jnp.float32
BF16 = jnp.bfloat16

EPS = 1e-6
ROPE_BASE = 10000.0
GRID_W = 64
N_MOD = 6
FOURIER_GROUPS = 4
ATTN_HEAD_DIM = 128
ATTN_KV_HEADS = 4

V7X_LANES = 128
V7X_VMEM_BYTES = 64 * 1024 * 1024
VMEM_LIMIT_BYTES = V7X_VMEM_BYTES - 8 * 1024 * 1024


def _tile(n, target, align=8):
    best = None
    for t in range(align, min(n, target) + 1, align):
        if n % t == 0:
            best = t
    return n if best is None else best


def _params(*semantics):
    return pltpu.CompilerParams(dimension_semantics=semantics, vmem_limit_bytes=VMEM_LIMIT_BYTES)


def _silu(x):
    return x * jax.nn.sigmoid(x)


def _norm_mod(x, shift, scale):
    ms = jnp.mean(x * x, axis=-1, keepdims=True)
    return x * lax.rsqrt(ms + EPS) * (1.0 + scale) + shift


def _next_tile(i, n_tiles):
    return jnp.minimum(i + 1, n_tiles - 1)


def _row_spec(tm, width, n_tiles=None):
    if n_tiles is None:
        return pl.BlockSpec((tm, width), lambda i, *_: (i, 0))
    return pl.BlockSpec((tm, width), lambda i, *_: (_next_tile(i, n_tiles), 0))


def _mod_spec(mod, rows_per_mod, tm, n_tiles=None):
    tiles = rows_per_mod // tm
    d = mod.shape[-1]
    if n_tiles is None:
        return pl.BlockSpec((None, 1, d), lambda i, *_: (i // tiles, 0, 0))
    return pl.BlockSpec((None, 1, d), lambda i, *_: (_next_tile(i, n_tiles) // tiles, 0, 0))


ROW_CHUNK = 16
LANE_CHUNK = 1024


def _modulate_rows(x_ref, sh_ref, sc_ref, dst, start=0, n_rows=None):
    n_rows = x_ref.shape[0] if n_rows is None else n_rows
    d = x_ref.shape[1]
    for r0 in range(0, n_rows, ROW_CHUNK):
        first = start + r0
        rows = pl.ds(first if isinstance(first, int) else pl.multiple_of(first, ROW_CHUNK), ROW_CHUNK)
        x = x_ref[rows, :]
        inv = lax.rsqrt(jnp.mean(x * x, axis=-1, keepdims=True) + EPS)
        for c0 in range(0, d, LANE_CHUNK):
            cols = slice(c0, min(c0 + LANE_CHUNK, d))
            y = x_ref[rows, cols] * inv * (1.0 + sc_ref[:, cols]) + sh_ref[:, cols]
            dst[rows, cols] = y.astype(BF16)


def _stage_modulated(x_ref, sh_ref, sc_ref, u_scr, slot, step, n_steps):
    tm = x_ref.shape[0]
    if n_steps == 1:
        _modulate_rows(x_ref, sh_ref, sc_ref, u_scr.at[slot])
    else:
        r = min(tm, -(-(-(-tm // n_steps)) // ROW_CHUNK) * ROW_CHUNK)
        start = pl.multiple_of(jnp.minimum(step * r, tm - r), ROW_CHUNK)
        _modulate_rows(x_ref, sh_ref, sc_ref, u_scr.at[slot], start, r)


def _mod_kernel(c_ref, w_ref, b_ref, o_ref):
    cond = _silu(c_ref[...]).astype(BF16)
    o_ref[...] = jnp.dot(cond, w_ref[...].astype(BF16), preferred_element_type=F32) + b_ref[...]


def _modulation(c_all, w_mod, b_mod):
    depth, d, n = w_mod.shape
    r = c_all.shape[0]
    tn = _tile(n, 1024, V7X_LANES)
    return pl.pallas_call(
        _mod_kernel,
        grid=(depth, n // tn),
        in_specs=[
            pl.BlockSpec((r, d), lambda i, j: (0, 0)),
            pl.BlockSpec((None, d, tn), lambda i, j: (i, 0, j)),
            pl.BlockSpec((None, 1, tn), lambda i, j: (i, 0, j)),
        ],
        out_specs=pl.BlockSpec((None, r, tn), lambda i, j: (i, 0, j)),
        out_shape=jax.ShapeDtypeStruct((depth, r, n), F32),
        compiler_params=_params("parallel", "parallel"),
        name="modulation",
    )(c_all, w_mod, b_mod.reshape(depth, 1, n))


def _cast_kernel(x_ref, o_ref):
    o_ref[...] = x_ref[...].astype(BF16)


def _to_bf16(w, layer, row_lo, row_hi):
    n = w.shape[2]
    rows = row_hi - row_lo
    tk = _tile(math.gcd(rows, row_lo), 256)
    tn = _tile(n, 4096, V7X_LANES)
    first = row_lo // tk
    return pl.pallas_call(
        _cast_kernel,
        grid=(rows // tk, n // tn),
        in_specs=[pl.BlockSpec((None, tk, tn), lambda r, c: (layer, first + r, c))],
        out_specs=pl.BlockSpec((None, tk, tn), lambda r, c: (0, r, c)),
        out_shape=jax.ShapeDtypeStruct((1, rows, n), BF16),
        compiler_params=_params("parallel", "parallel"),
        name="to_bf16",
    )(w)


def _cast_plan(casts, grid):
    steps = math.prod(grid)

    def step_of(*idx):
        s = 0
        for i, g in zip(idx, grid):
            s = s * g + i
        return s

    in_specs, out_specs, out_shapes = [], [], []
    for w, layer, lo, hi in casts:
        rows, n = hi - lo, w.shape[2]
        unit = math.gcd(rows, lo)
        rb = min(t for t in range(16, unit + 1, 16) if unit % t == 0 and rows // t <= steps)
        nb, first = rows // rb, lo // rb
        in_specs.append(pl.BlockSpec(
            (None, rb, n),
            lambda *idx, layer=layer, first=first, nb=nb: (layer, first + jnp.minimum(step_of(*idx), nb - 1), 0)))
        out_specs.append(pl.BlockSpec(
            (None, rb, n), lambda *idx, nb=nb: (0, jnp.minimum(step_of(*idx), nb - 1), 0)))
        out_shapes.append(jax.ShapeDtypeStruct((1, rows, n), BF16))
    return in_specs, out_specs, out_shapes


def _run_casts(in_refs, out_refs):
    for x_ref, o_ref in zip(in_refs, out_refs):
        o_ref[...] = x_ref[...].astype(BF16)


def _rope_angles(n_tokens, dp):
    t = np.arange(n_tokens)
    inv = ROPE_BASE ** (-np.arange(0, dp, 2, dtype=np.float64) / dp)
    return (t // GRID_W)[:, None] * inv[None, :], (t % GRID_W)[:, None] * inv[None, :]


def _rope_tables(n_tokens, head_dim, identity):
    half = head_dim // 2
    if identity:
        return jnp.ones((n_tokens, head_dim), F32), jnp.zeros((n_tokens, head_dim), F32)
    row, col = _rope_angles(n_tokens, half)
    cos = np.concatenate([np.cos(row), np.cos(row), np.cos(col), np.cos(col)], axis=-1)
    sin = np.concatenate([-np.sin(row), np.sin(row), -np.sin(col), np.sin(col)], axis=-1)
    return jnp.asarray(cos, F32), jnp.asarray(sin, F32)


def _rope_apply(y, cos, sin, quarter):
    if 2 * quarter == V7X_LANES:
        return y * cos + pltpu.roll(y, quarter, 1) * sin
    lane = lax.broadcasted_iota(jnp.int32, y.shape, 1)
    first = (lane % (2 * quarter)) < quarter
    partner = jnp.where(first, pltpu.roll(y, V7X_LANES - quarter, 1), pltpu.roll(y, quarter, 1))
    return y * cos + partner * sin


def _proj_even_kernel(x_ref, sh_ref, sc_ref, xn_ref, shn_ref, scn_ref, w_ref, qg_ref, kg_ref, cos_ref, sin_ref,
                      f_ref, q_ref, k_ref, v_ref, u_scr, *, fdim, qdim, kvdim, hd):
    i = pl.program_id(0)
    slot = i % 2

    @pl.when(i == 0)
    def _():
        _stage_modulated(x_ref, sh_ref, sc_ref, u_scr, 0, 0, 1)

    u = u_scr[slot]
    cos = cos_ref[...]
    sin = sin_ref[...]

    def heads(lo, width, gain_ref, out_ref):
        seg = _tile(width, 512, hd)
        gain = gain_ref[...]
        for s in range(width // seg):
            y = jnp.dot(u, w_ref[:, lo + s * seg:lo + (s + 1) * seg], preferred_element_type=F32)
            for h in range(seg // hd):
                yh = y[:, h * hd:(h + 1) * hd]
                ms = jnp.mean(yh * yh, axis=-1, keepdims=True)
                yh = yh * lax.rsqrt(ms + EPS) * gain
                yh = _rope_apply(yh, cos, sin, hd // 4)
                out_ref[:, s * seg + h * hd:s * seg + (h + 1) * hd] = yh.astype(BF16)

    heads(fdim, qdim, qg_ref, q_ref)
    heads(fdim + qdim, kvdim, kg_ref, k_ref)
    f_ref[...] = jnp.dot(u, w_ref[:, :fdim], preferred_element_type=F32).astype(BF16)
    v_ref[...] = jnp.dot(u, w_ref[:, fdim + qdim + kvdim:], preferred_element_type=F32).astype(BF16)
    _stage_modulated(xn_ref, shn_ref, scn_ref, u_scr, 1 - slot, 0, 1)


def _first_tile_specs(tm, d, shift, scale):
    once = pl.Buffered(1)
    return [pl.BlockSpec((tm, d), lambda i, *_: (0, 0), pipeline_mode=once),
            pl.BlockSpec((None, 1, d), lambda i, *_: (0, 0, 0), pipeline_mode=once),
            pl.BlockSpec((None, 1, d), lambda i, *_: (0, 0, 0), pipeline_mode=once)]


def _proj_even(h, shift, scale, rows_per_mod, w, q_gain, k_gain, cos, sin, dims):
    n, d = h.shape
    fdim, qdim, kvdim = dims
    hd = ATTN_HEAD_DIM
    pos_rows = cos.shape[0]
    tm = _tile(math.gcd(rows_per_mod, pos_rows), 512)
    n_tiles = n // tm
    pos_tiles = pos_rows // tm
    kern = functools.partial(_proj_even_kernel, fdim=fdim, qdim=qdim, kvdim=kvdim, hd=hd)
    const = lambda a: pl.BlockSpec(a.shape, lambda i: (0,) * a.ndim)
    return pl.pallas_call(
        kern,
        grid=(n_tiles,),
        in_specs=_first_tile_specs(tm, d, shift, scale) + [
            _row_spec(tm, d, n_tiles),
            _mod_spec(shift, rows_per_mod, tm, n_tiles), _mod_spec(scale, rows_per_mod, tm, n_tiles),
            pl.BlockSpec((None,) + w.shape[1:], lambda i: (0, 0, 0)),
            const(q_gain), const(k_gain),
            pl.BlockSpec((tm, hd), lambda i: (i % pos_tiles, 0)),
            pl.BlockSpec((tm, hd), lambda i: (i % pos_tiles, 0)),
        ],
        out_specs=[_row_spec(tm, fdim), _row_spec(tm, qdim), _row_spec(tm, kvdim), _row_spec(tm, kvdim)],
        out_shape=[jax.ShapeDtypeStruct((n, fdim), BF16), jax.ShapeDtypeStruct((n, qdim), BF16),
                   jax.ShapeDtypeStruct((n, kvdim), BF16), jax.ShapeDtypeStruct((n, kvdim), BF16)],
        scratch_shapes=[pltpu.VMEM((2, tm, d), BF16)],
        compiler_params=_params("arbitrary"),
        name="proj_even",
    )(h, shift, scale, h, shift, scale, w, q_gain, k_gain, cos, sin)


def _attn_kernel(q_ref, *refs, n_seg, n_cast, group, hd, scale, n_split):
    k_refs, v_refs = refs[:n_seg], refs[n_seg:2 * n_seg]
    cast_in, refs = refs[2 * n_seg:2 * n_seg + n_cast], refs[2 * n_seg + n_cast:]
    o_ref, cast_out = refs[0], refs[1:1 + n_cast]
    s_scr, p_scr, v_scr = refs[1 + n_cast:]
    tq = q_ref.shape[0]
    _run_casts(cast_in, cast_out)

    @pl.when(pl.program_id(2) == 0)
    def _():
        off = 0
        for v in v_refs:
            n = v.shape[0]
            v_scr[off:off + n, :hd] = v[...]
            v_scr[off:off + n, hd:] = jnp.ones((n, hd), BF16)
            off += n

    q = q_ref[...]
    qs = jnp.concatenate([q[:, g * hd:(g + 1) * hd] for g in range(group)], axis=0)
    nt = (((1,), (1,)), ((), ()))
    rows = qs.shape[0] // n_split
    blocks = [slice(i * rows, (i + 1) * rows) for i in range(n_split)]
    for blk in blocks:
        off = 0
        for k in k_refs:
            n = k.shape[0]
            s_scr[blk, off:off + n] = lax.dot_general(qs[blk], k[...], nt, preferred_element_type=F32)
            off += n
    outs = []
    for blk in blocks:
        m = jnp.max(s_scr[blk, :], axis=-1, keepdims=True)
        p_scr[blk, :] = jnp.exp2((s_scr[blk, :] - m) * (scale * math.log2(math.e))).astype(BF16)
    for blk in blocks:
        o = jnp.dot(p_scr[blk, :], v_scr[...], preferred_element_type=F32)
        outs.append(o[:, :hd] / o[:, hd:])
    o = jnp.concatenate(outs, axis=0)
    for g in range(group):
        o_ref[:, g * hd:(g + 1) * hd] = o[g * tq:(g + 1) * tq].astype(BF16)


def _attention(q, segments, batch, casts=()):
    hd, kvh = ATTN_HEAD_DIM, ATTN_KV_HEADS
    n, qdim = q.shape
    group = qdim // (kvh * hd)
    t_q = n // batch
    tq = _tile(t_q, 512)
    nq = t_q // tq
    n_keys = sum(k.shape[0] for k, _ in segments) // batch
    grid = (batch, kvh, nq)
    cast_in, cast_out, cast_shapes = _cast_plan(casts, grid)
    kern = functools.partial(_attn_kernel, n_seg=len(segments), n_cast=len(casts), group=group, hd=hd,
                             scale=hd ** -0.5, n_split=4)
    kv_spec = lambda a: pl.BlockSpec((a.shape[0] // batch, hd), lambda b, kh, t: (b, kh))
    q_spec = pl.BlockSpec((tq, group * hd), lambda b, kh, t: (b * nq + t, kh))
    out = pl.pallas_call(
        kern,
        grid=grid,
        in_specs=[q_spec] + [kv_spec(k) for k, _ in segments] + [kv_spec(v) for _, v in segments] + cast_in,
        out_specs=[q_spec] + cast_out,
        out_shape=[jax.ShapeDtypeStruct((n, qdim), BF16)] + cast_shapes,
        scratch_shapes=[pltpu.VMEM((group * tq, n_keys), F32), pltpu.VMEM((group * tq, n_keys), BF16),
                        pltpu.VMEM((n_keys, 2 * hd), BF16)],
        compiler_params=_params("arbitrary", "arbitrary", "arbitrary"),
        name="attention",
    )(q, *[k for k, _ in segments], *[v for _, v in segments], *[w[0] for w in casts])
    return out[0], out[1:]


def _fourier_kernel(u_ref, cm_ref, cs_ref, o_ref, z_scr, *, groups, gd, scale):
    t = u_ref.shape[0]

    @pl.when(pl.program_id(1) == 0)
    def _():
        for g in range(groups):
            cols = slice(g * gd, (g + 1) * gd)
            z = jnp.dot(u_ref[:, cols], cm_ref[...], preferred_element_type=F32)
            z_scr[0:t, cols] = z[:, :gd].astype(BF16)
            z_scr[t:2 * t, cols] = z[:, gd:].astype(BF16)

    y = jnp.dot(cs_ref[...], z_scr[...], preferred_element_type=F32)
    o_ref[...] = (y * scale).astype(BF16)


def _dft_tables(t, gd):
    def angles(n):
        i = np.arange(n)
        return ((i[:, None] * i[None, :]) % n) * (2.0 * math.pi / n)
    a_c, a_t = angles(gd), angles(t)
    cm = np.concatenate([np.cos(a_c), -np.sin(a_c)], axis=1)
    cs = np.concatenate([np.cos(a_t), np.sin(a_t)], axis=1)
    return jnp.asarray(cm.astype(BF16)), jnp.asarray(cs.astype(BF16))


def _fourier(u, batch, tables):
    n, fdim = u.shape
    t = n // batch
    gd = fdim // FOURIER_GROUPS
    cm, cs = tables
    tr = _tile(t, 512)
    kern = functools.partial(_fourier_kernel, groups=FOURIER_GROUPS, gd=gd, scale=(t * gd) ** -0.5)
    return pl.pallas_call(
        kern,
        grid=(batch, t // tr),
        in_specs=[
            pl.BlockSpec((t, fdim), lambda b, r: (b, 0)),
            pl.BlockSpec(cm.shape, lambda b, r: (0, 0)),
            pl.BlockSpec((tr, 2 * t), lambda b, r: (r, 0)),
        ],
        out_specs=pl.BlockSpec((tr, fdim), lambda b, r: (b * (t // tr) + r, 0)),
        out_shape=jax.ShapeDtypeStruct((n, fdim), BF16),
        scratch_shapes=[pltpu.VMEM((2 * t, fdim), BF16)],
        compiler_params=_params("parallel", "arbitrary"),
        name="fourier",
    )(u, cm, cs)


def _proj_odd_kernel(x_ref, sh_ref, sc_ref, xn_ref, shn_ref, scn_ref, w_ref, cos_ref, sin_ref, o_ref, u_scr,
                     *, n_steps, q_tiles, qk_tiles, hd, k_scale):
    i, j = pl.program_id(0), pl.program_id(1)
    slot = i % 2

    @pl.when((i == 0) & (j == 0))
    def _():
        _stage_modulated(x_ref, sh_ref, sc_ref, u_scr, 0, 0, 1)

    y = jnp.dot(u_scr[slot], w_ref[...], preferred_element_type=F32)
    tn = y.shape[1]
    _stage_modulated(xn_ref, shn_ref, scn_ref, u_scr, 1 - slot, j, n_steps)

    @pl.when(j < qk_tiles)
    def _():
        mult = jnp.where(j < q_tiles, 1.0, k_scale).astype(F32)
        for c in range(tn // V7X_LANES):
            lanes = slice(c * V7X_LANES, (c + 1) * V7X_LANES)
            tab = slice((c % (hd // V7X_LANES)) * V7X_LANES, (c % (hd // V7X_LANES) + 1) * V7X_LANES)
            r = _rope_apply(y[:, lanes], cos_ref[:, tab], sin_ref[:, tab], hd // 4)
            o_ref[:, lanes] = (r * mult).astype(BF16)

    @pl.when(j >= qk_tiles)
    def _():
        o_ref[...] = y.astype(BF16)


def _proj_odd(h, shift, scale, rows_per_mod, w, cos, sin, qk_dim):
    n, d = h.shape
    n_out = w.shape[2]
    hd = cos.shape[1]
    pos_rows = cos.shape[0]
    tm = _tile(math.gcd(rows_per_mod, pos_rows), 1024)
    tn = _tile(math.gcd(qk_dim, n_out), 1024, hd)
    n_tiles = n // tm
    pos_tiles = pos_rows // tm
    kern = functools.partial(_proj_odd_kernel, n_steps=n_out // tn, q_tiles=qk_dim // tn,
                             qk_tiles=2 * qk_dim // tn, hd=hd, k_scale=hd ** -0.5)
    return pl.pallas_call(
        kern,
        grid=(n_tiles, n_out // tn),
        in_specs=_first_tile_specs(tm, d, shift, scale) + [
            _row_spec(tm, d, n_tiles),
            _mod_spec(shift, rows_per_mod, tm, n_tiles), _mod_spec(scale, rows_per_mod, tm, n_tiles),
            pl.BlockSpec((None, d, tn), lambda i, j: (0, 0, j)),
            pl.BlockSpec((tm, hd), lambda i, j: (i % pos_tiles, 0)),
            pl.BlockSpec((tm, hd), lambda i, j: (i % pos_tiles, 0)),
        ],
        out_specs=pl.BlockSpec((tm, tn), lambda i, j: (i, j)),
        out_shape=jax.ShapeDtypeStruct((n, n_out), BF16),
        scratch_shapes=[pltpu.VMEM((2, tm, d), BF16)],
        compiler_params=_params("arbitrary", "arbitrary"),
        name="proj_odd",
    )(h, shift, scale, h, shift, scale, w, cos, sin)


def _retention_kernel(lgf_ref, lgb_ref, qc_ref, kc_ref, vc_ref, gc_ref, ql_ref, kl_ref, vl_ref, gl_ref,
                      *refs, chunk, n_cast):
    cast_in, refs = refs[:n_cast], refs[n_cast:]
    oc_ref, ol_ref = refs[:2]
    cast_out = refs[2:2 + n_cast]
    sf_scr, sb_scr, d_scr, dec_scr, partc_scr, partl_scr = refs[2 + n_cast:]
    _run_casts(cast_in, cast_out)
    c = chunk
    dk = sf_scr.shape[0]
    lgf = lgf_ref[...][:, :1]
    lgb = lgb_ref[...][:, :1]
    diff = (lax.broadcasted_iota(jnp.int32, (c, c), 0)
            - lax.broadcasted_iota(jnp.int32, (c, c), 1)).astype(F32)
    d_scr[...] = jnp.where(diff >= 0.0, jnp.exp(jnp.maximum(diff, 0.0) * lgf),
                           jnp.exp(jnp.maximum(-diff, 0.0) * lgb))
    idx = lax.broadcasted_iota(jnp.int32, (c, dk), 0).astype(F32)
    dec_scr[0] = jnp.exp((idx + 1.0) * lgf).astype(BF16)
    dec_scr[1] = jnp.exp((c - 1.0 - idx) * lgf).astype(BF16)
    dec_scr[2] = jnp.exp((c - idx) * lgb).astype(BF16)
    dec_scr[3] = jnp.exp(idx * lgb).astype(BF16)
    chunk_dec_f, chunk_dec_b = jnp.exp(c * lgf), jnp.exp(c * lgb)
    nt = (((1,), (1,)), ((), ()))
    tn = (((0,), (0,)), ((), ()))

    sf_scr[...] = jnp.zeros_like(sf_scr)
    sb_scr[...] = jnp.zeros_like(sb_scr)

    def forward(seg, i):
        q_ref, k_ref, v_ref = seg[:3]
        rows = slice(i * c, (i + 1) * c)
        q, k, v = q_ref[rows, :], k_ref[rows, :], v_ref[rows, :]
        s = lax.dot_general(q, k, nt, preferred_element_type=F32) * d_scr[...]
        lhs = jnp.concatenate([s.astype(BF16), q * dec_scr[0]], axis=1)
        rhs = jnp.concatenate([v, sf_scr[...].astype(BF16)], axis=0)
        o = jnp.dot(lhs, rhs, preferred_element_type=F32)
        sf_scr[...] = sf_scr[...] * chunk_dec_f + lax.dot_general(k * dec_scr[1], v, tn,
                                                                  preferred_element_type=F32)
        return o

    def backward(seg, i):
        q_ref, k_ref, v_ref = seg[:3]
        rows = slice(i * c, (i + 1) * c)
        q, k, v = q_ref[rows, :], k_ref[rows, :], v_ref[rows, :]
        o = jnp.dot(q * dec_scr[2], sb_scr[...].astype(BF16), preferred_element_type=F32)
        sb_scr[...] = sb_scr[...] * chunk_dec_b + lax.dot_general(k * dec_scr[3], v, tn,
                                                                  preferred_element_type=F32)
        return o

    def finish(seg, i, o):
        g_ref, o_ref = seg[3], seg[4]
        rows = slice(i * c, (i + 1) * c)
        o = o * lax.rsqrt(jnp.mean(o * o, axis=-1, keepdims=True) + EPS)
        o_ref[rows, :] = _silu(g_ref[rows, :]) * o.astype(BF16)

    def sweep(seg):
        part = seg[5]
        n = seg[0].shape[0] // c
        for t in range(n):
            i_f, i_b = t, n - 1 - t
            o_f, o_b = forward(seg, i_f), backward(seg, i_b)
            if i_f == i_b:
                finish(seg, i_f, o_f + o_b)
            elif i_f < i_b:
                part[i_f * c:(i_f + 1) * c, :] = o_f
                part[i_b * c:(i_b + 1) * c, :] = o_b
            else:
                finish(seg, i_f, o_f + part[i_f * c:(i_f + 1) * c, :])
                finish(seg, i_b, o_b + part[i_b * c:(i_b + 1) * c, :])

    sweep((qc_ref, kc_ref, vc_ref, gc_ref, oc_ref, partc_scr))
    sweep((ql_ref, kl_ref, vl_ref, gl_ref, ol_ref, partl_scr))


def _retention(p_ctx, p_lat, lg_fwd, lg_bwd, batch, casts=()):
    heads = lg_fwd.shape[0]
    n_out = p_lat.shape[1]
    dk = n_out // (6 * heads)
    dv = 2 * dk
    t_c, t_l = p_ctx.shape[0] // batch, p_lat.shape[0] // batch
    c = _tile(math.gcd(t_c, t_l), 256)
    lanes = lambda lg: jnp.broadcast_to(lg.astype(F32)[:, None, None], (heads, 1, V7X_LANES))
    lg_spec = pl.BlockSpec((None, 1, V7X_LANES), lambda b, h: (h, 0, 0))

    def specs(t):
        return [pl.BlockSpec((t, dk), lambda b, h: (b, h)),
                pl.BlockSpec((t, dk), lambda b, h: (b, heads + h)),
                pl.BlockSpec((t, dv), lambda b, h: (b, heads + h)),
                pl.BlockSpec((t, dv), lambda b, h: (b, 2 * heads + h))]

    out_spec = lambda t: pl.BlockSpec((t, dv), lambda b, h: (b, h))
    grid = (batch, heads)
    cast_in, cast_out, cast_shapes = _cast_plan(casts, grid)
    out = pl.pallas_call(
        functools.partial(_retention_kernel, chunk=c, n_cast=len(casts)),
        grid=grid,
        in_specs=[lg_spec, lg_spec] + specs(t_c) + specs(t_l) + cast_in,
        out_specs=[out_spec(t_c), out_spec(t_l)] + cast_out,
        out_shape=[jax.ShapeDtypeStruct((p_ctx.shape[0], heads * dv), BF16),
                   jax.ShapeDtypeStruct((p_lat.shape[0], heads * dv), BF16)] + cast_shapes,
        scratch_shapes=[pltpu.VMEM((dk, dv), F32), pltpu.VMEM((dk, dv), F32), pltpu.VMEM((c, c), F32),
                        pltpu.VMEM((4, c, dk), BF16),
                        pltpu.VMEM((t_c, dv), F32), pltpu.VMEM((t_l, dv), F32)],
        compiler_params=_params("arbitrary", "arbitrary"),
        name="retention",
    )(lanes(lg_fwd), lanes(lg_bwd), p_ctx, p_ctx, p_ctx, p_ctx, p_lat, p_lat, p_lat, p_lat,
      *[w[0] for w in casts])
    return out[0], out[1], out[2:]


def _outproj_kernel(*refs, n_parts, seg):
    a_refs, w_refs = refs[:n_parts], refs[n_parts:2 * n_parts]
    res_ref, gate_ref, o_ref = refs[2 * n_parts:]
    for c0 in range(0, o_ref.shape[1], seg):
        cols = slice(c0, c0 + seg)
        acc = functools.reduce(jnp.add, [jnp.dot(a[...], w[:, cols], preferred_element_type=F32)
                                         for a, w in zip(a_refs, w_refs)])
        o_ref[:, cols] = res_ref[:, cols] + gate_ref[:, cols] * acc


def _outproj(parts, weights, res, gate, rows_per_mod):
    n, d = res.shape
    tm = _tile(rows_per_mod, 512)
    return pl.pallas_call(
        functools.partial(_outproj_kernel, n_parts=len(parts), seg=_tile(d, 512, V7X_LANES)),
        grid=(n // tm,),
        in_specs=[_row_spec(tm, a.shape[1]) for a in parts]
        + [pl.BlockSpec((None, w.shape[1], d), lambda i: (0, 0, 0), pipeline_mode=pl.Buffered(1))
           for w in weights]
        + [_row_spec(tm, d), _mod_spec(gate, rows_per_mod, tm)],
        out_specs=_row_spec(tm, d),
        out_shape=jax.ShapeDtypeStruct((n, d), F32),
        compiler_params=_params("parallel"),
        name="outproj",
    )(*parts, *weights, res, gate)


def _ffn_kernel(x_ref, sh_ref, sc_ref, gate_ref, wg_ref, wu_ref, wo_ref, o_ref, u_scr, *, n_steps):
    j = pl.program_id(1)

    @pl.when(j == 0)
    def _():
        _modulate_rows(x_ref, sh_ref, sc_ref, u_scr)
        o_ref[...] = jnp.zeros_like(o_ref)

    u = u_scr[...]
    g = jnp.dot(u, wg_ref[...], preferred_element_type=F32)
    up = jnp.dot(u, wu_ref[...], preferred_element_type=F32)
    act = (_silu(g) * up).astype(BF16)
    o_ref[...] += jnp.dot(act, wo_ref[...], preferred_element_type=F32)

    @pl.when(j == n_steps - 1)
    def _():
        o_ref[...] = x_ref[...] + gate_ref[...] * o_ref[...]


def _ffn(h, shift, scale, gate, rows_per_mod, w_in, w_out):
    n, d = h.shape
    f = w_out.shape[1]
    tm = _tile(rows_per_mod, 1024)
    tf = _tile(f, 512, V7X_LANES)
    nf = f // tf
    n_tiles = n // tm
    return pl.pallas_call(
        functools.partial(_ffn_kernel, n_steps=nf),
        grid=(n_tiles, nf),
        in_specs=[
            _row_spec(tm, d),
            _mod_spec(shift, rows_per_mod, tm), _mod_spec(scale, rows_per_mod, tm),
            _mod_spec(gate, rows_per_mod, tm),
            pl.BlockSpec((None, d, tf), lambda i, j: (0, 0, j)),
            pl.BlockSpec((None, d, tf), lambda i, j: (0, 0, nf + j)),
            pl.BlockSpec((None, tf, d), lambda i, j: (0, j, 0)),
        ],
        out_specs=_row_spec(tm, d),
        out_shape=jax.ShapeDtypeStruct((n, d), F32),
        scratch_shapes=[pltpu.VMEM((tm, d), BF16)],
        compiler_params=_params("parallel", "arbitrary"),
        name="ffn",
    )(h, shift, scale, gate, w_in, w_in, w_out)


def _even_layer(h_ctx, h_lat, mod_ctx, mod_lat, weights, q_gain, k_gain, batch, need_ctx, tables, casts):
    n_ctx = h_ctx.shape[0]
    n_lat = h_lat.shape[0]
    hd = ATTN_HEAD_DIM
    w_in, w_f, w_a = weights
    fdim = w_f.shape[1]
    kvdim = ATTN_KV_HEADS * hd
    qdim = w_in.shape[2] - fdim - 2 * kvdim
    dims = (fdim, qdim, kvdim)
    q_gain, k_gain = q_gain.reshape(1, hd), k_gain.reshape(1, hd)

    f_c, q_c, k_c, v_c = _proj_even(h_ctx, mod_ctx[0], mod_ctx[1], n_ctx, w_in, q_gain, k_gain,
                                    *tables["rope_even_ctx"], dims)
    f_l, q_l, k_l, v_l = _proj_even(h_lat, mod_lat[0], mod_lat[1], n_lat // batch, w_in, q_gain, k_gain,
                                    *tables["rope_even_lat"], dims)
    a_l, next_weights = _attention(q_l, [(k_l, v_l), (k_c, v_c)], batch, casts)
    m_l = _fourier(f_l, batch, tables["dft_lat"])
    h_lat = _outproj([m_l, a_l], [w_f, w_a], h_lat, mod_lat[2], n_lat // batch)
    if need_ctx:
        a_c, _ = _attention(q_c, [(k_c, v_c)], batch)
        m_c = _fourier(f_c, batch, tables["dft_ctx"])
        h_ctx = _outproj([m_c, a_c], [w_f, w_a], h_ctx, mod_ctx[2], n_ctx)
    return h_ctx, h_lat, next_weights


def _odd_layer(h_ctx, h_lat, mod_ctx, mod_lat, weights, lg_fwd, lg_bwd, batch, need_ctx, tables, casts):
    n_ctx = h_ctx.shape[0]
    n_lat = h_lat.shape[0]
    w_in, w_out = weights
    qk_dim = w_in.shape[2] // 6
    p_c = _proj_odd(h_ctx, mod_ctx[0], mod_ctx[1], n_ctx, w_in, *tables["rope_odd_ctx"], qk_dim)
    p_l = _proj_odd(h_lat, mod_lat[0], mod_lat[1], n_lat // batch, w_in, *tables["rope_odd_lat"], qk_dim)
    m_c, m_l, next_weights = _retention(p_c, p_l, lg_fwd, lg_bwd, batch, casts)
    h_lat = _outproj([m_l], [w_out], h_lat, mod_lat[2], n_lat // batch)
    if need_ctx:
        h_ctx = _outproj([m_c], [w_out], h_ctx, mod_ctx[2], n_ctx)
    return h_ctx, h_lat, next_weights


def kernel(x, c, ctx, c_ctx, w_mod, b_mod, w_in_even, w_out_even, q_gain_even, k_gain_even,
           w_in_odd, w_out_odd, log_decay_fwd, log_decay_bwd, w_ffn_in, w_ffn_out):
    batch, seq, d = x.shape
    ctx_len = ctx.shape[1]
    depth = w_mod.shape[0]
    n_lat, n_ctx = batch * seq, batch * ctx_len
    h_lat, h_ctx = x.reshape(n_lat, d), ctx.reshape(n_ctx, d)

    rows = -(-(batch + 1) // 8) * 8
    c_all = jnp.concatenate([c, c_ctx[None], jnp.zeros((rows - batch - 1, d), F32)], axis=0)
    mods = _modulation(c_all, w_mod, b_mod).reshape(depth, rows, N_MOD, d)

    fdim = FOURIER_GROUPS * (d // 16)

    def weight_slices(i):
        j = i // 2
        if i % 2 == 0:
            mixer = [(w_in_even, j, 0, d), (w_out_even, j, 0, fdim), (w_out_even, j, fdim, w_out_even.shape[1])]
        else:
            mixer = [(w_in_odd, j, 0, d), (w_out_odd, j, 0, w_out_odd.shape[1])]
        return mixer + [(w_ffn_in, i, 0, d), (w_ffn_out, i, 0, w_ffn_out.shape[1])]

    ret_hd = w_in_odd.shape[2] // (6 * log_decay_fwd.shape[1])
    ctx_rows = _tile(n_ctx, 1024)
    tables = {
        "rope_even_lat": _rope_tables(seq, ATTN_HEAD_DIM, False),
        "rope_even_ctx": _rope_tables(ctx_rows, ATTN_HEAD_DIM, True),
        "rope_odd_lat": _rope_tables(seq, ret_hd, False),
        "rope_odd_ctx": _rope_tables(ctx_rows, ret_hd, True),
        "dft_lat": _dft_tables(seq, d // 16),
        "dft_ctx": _dft_tables(ctx_len, d // 16),
    }

    weights = [_to_bf16(*s) for s in weight_slices(0)]
    for i in range(depth):
        need_ctx = i < depth - 1
        casts = weight_slices(i + 1) if i + 1 < depth else ()
        mod_lat = [mods[i, :batch, k][:, None, :] for k in range(N_MOD)]
        mod_ctx = [mods[i, batch:batch + 1, k][:, None, :] for k in range(N_MOD)]
        j = i // 2
        if i % 2 == 0:
            h_ctx, h_lat, next_weights = _even_layer(h_ctx, h_lat, mod_ctx, mod_lat, weights[:-2],
                                                     q_gain_even[j], k_gain_even[j], batch, need_ctx, tables, casts)
        else:
            h_ctx, h_lat, next_weights = _odd_layer(h_ctx, h_lat, mod_ctx, mod_lat, weights[:-2],
                                                    log_decay_fwd[j], log_decay_bwd[j], batch, need_ctx, tables, casts)
        h_lat = _ffn(h_lat, mod_lat[3], mod_lat[4], mod_lat[5], n_lat // batch, *weights[-2:])
        if need_ctx:
            h_ctx = _ffn(h_ctx, mod_ctx[3], mod_ctx[4], mod_ctx[5], n_ctx, *weights[-2:])
        weights = list(next_weights)
    return h_lat.reshape(batch, seq, d)
```

```python
import functools
import math

import jax
import jax.numpy as jnp
import numpy as np
from jax import lax
from jax.experimental import pallas as pl
from jax.experimental.pallas import tpu as pltpu

F32 = jnp.float32
BF16 = jnp.bfloat16

EPS = 1e-6
ROPE_BASE = 10000.0
GRID_W = 64
N_MOD = 6
FOURIER_GROUPS = 4
ATTN_HEAD_DIM = 128
ATTN_KV_HEADS = 4

V7X_LANES = 128
V7X_VMEM_BYTES = 64 * 1024 * 1024
VMEM_LIMIT_BYTES = V7X_VMEM_BYTES - 8 * 1024 * 1024


def _tile(n, target, align=8):
    best = None
    for t in range(align, min(n, target) + 1, align):
        if n % t == 0:
            best = t
    return n if best is None else best


def _params(*semantics):
    return pltpu.CompilerParams(dimension_semantics=semantics, vmem_limit_bytes=VMEM_LIMIT_BYTES)


def _silu(x):
    return x * jax.nn.sigmoid(x)


def _norm_mod(x, shift, scale):
    ms = jnp.mean(x * x, axis=-1, keepdims=True)
    return x * lax.rsqrt(ms + EPS) * (1.0 + scale) + shift


def _next_tile(i, n_tiles):
    return jnp.minimum(i + 1, n_tiles - 1)


def _row_spec(tm, width, n_tiles=None):
    if n_tiles is None:
        return pl.BlockSpec((tm, width), lambda i, *_: (i, 0))
    return pl.BlockSpec((tm, width), lambda i, *_: (_next_tile(i, n_tiles), 0))


def _mod_spec(mod, rows_per_mod, tm, n_tiles=None):
    tiles = rows_per_mod // tm
    d = mod.shape[-1]
    if n_tiles is None:
        return pl.BlockSpec((None, 1, d), lambda i, *_: (i // tiles, 0, 0))
    return pl.BlockSpec((None, 1, d), lambda i, *_: (_next_tile(i, n_tiles) // tiles, 0, 0))


ROW_CHUNK = 16
LANE_CHUNK = 1024


def _modulate_rows(x_ref, sh_ref, sc_ref, dst, start=0, n_rows=None):
    n_rows = x_ref.shape[0] if n_rows is None else n_rows
    d = x_ref.shape[1]
    for r0 in range(0, n_rows, ROW_CHUNK):
        first = start + r0
        rows = pl.ds(first if isinstance(first, int) else pl.multiple_of(first, ROW_CHUNK), ROW_CHUNK)
        x = x_ref[rows, :]
        inv = lax.rsqrt(jnp.mean(x * x, axis=-1, keepdims=True) + EPS)
        for c0 in range(0, d, LANE_CHUNK):
            cols = slice(c0, min(c0 + LANE_CHUNK, d))
            y = x_ref[rows, cols] * inv * (1.0 + sc_ref[:, cols]) + sh_ref[:, cols]
            dst[rows, cols] = y.astype(BF16)


def _stage_modulated(x_ref, sh_ref, sc_ref, u_scr, slot, step, n_steps):
    tm = x_ref.shape[0]
    if n_steps == 1:
        _modulate_rows(x_ref, sh_ref, sc_ref, u_scr.at[slot])
    else:
        r = min(tm, -(-(-(-tm // n_steps)) // ROW_CHUNK) * ROW_CHUNK)
        start = pl.multiple_of(jnp.minimum(step * r, tm - r), ROW_CHUNK)
        _modulate_rows(x_ref, sh_ref, sc_ref, u_scr.at[slot], start, r)


def _mod_kernel(c_ref, w_ref, b_ref, o_ref):
    cond = _silu(c_ref[...]).astype(BF16)
    o_ref[...] = jnp.dot(cond, w_ref[...].astype(BF16), preferred_element_type=F32) + b_ref[...]


def _modulation(c_all, w_mod, b_mod):
    depth, d, n = w_mod.shape
    r = c_all.shape[0]
    tn = _tile(n, 1024, V7X_LANES)
    return pl.pallas_call(
        _mod_kernel,
        grid=(depth, n // tn),
        in_specs=[
            pl.BlockSpec((r, d), lambda i, j: (0, 0)),
            pl.BlockSpec((None, d, tn), lambda i, j: (i, 0, j)),
            pl.BlockSpec((None, 1, tn), lambda i, j: (i, 0, j)),
        ],
        out_specs=pl.BlockSpec((None, r, tn), lambda i, j: (i, 0, j)),
        out_shape=jax.ShapeDtypeStruct((depth, r, n), F32),
        compiler_params=_params("parallel", "parallel"),
        name="modulation",
    )(c_all, w_mod, b_mod.reshape(depth, 1, n))


def _cast_kernel(x_ref, o_ref):
    o_ref[...] = x_ref[...].astype(BF16)


def _to_bf16(w, layer, row_lo, row_hi):
    n = w.shape[2]
    rows = row_hi - row_lo
    tk = _tile(math.gcd(rows, row_lo), 256)
    tn = _tile(n, 4096, V7X_LANES)
    first = row_lo // tk
    return pl.pallas_call(
        _cast_kernel,
        grid=(rows // tk, n // tn),
        in_specs=[pl.BlockSpec((None, tk, tn), lambda r, c: (layer, first + r, c))],
        out_specs=pl.BlockSpec((None, tk, tn), lambda r, c: (0, r, c)),
        out_shape=jax.ShapeDtypeStruct((1, rows, n), BF16),
        compiler_params=_params("parallel", "parallel"),
        name="to_bf16",
    )(w)


def _cast_plan(casts, grid):
    steps = math.prod(grid)

    def step_of(*idx):
        s = 0
        for i, g in zip(idx, grid):
            s = s * g + i
        return s

    in_specs, out_specs, out_shapes = [], [], []
    for w, layer, lo, hi in casts:
        rows, n = hi - lo, w.shape[2]
        unit = math.gcd(rows, lo)
        rb = min(t for t in range(16, unit + 1, 16) if unit % t == 0 and rows // t <= steps)
        nb, first = rows // rb, lo // rb
        in_specs.append(pl.BlockSpec(
            (None, rb, n),
            lambda *idx, layer=layer, first=first, nb=nb: (layer, first + jnp.minimum(step_of(*idx), nb - 1), 0)))
        out_specs.append(pl.BlockSpec(
            (None, rb, n), lambda *idx, nb=nb: (0, jnp.minimum(step_of(*idx), nb - 1), 0)))
        out_shapes.append(jax.ShapeDtypeStruct((1, rows, n), BF16))
    return in_specs, out_specs, out_shapes


def _run_casts(in_refs, out_refs):
    for x_ref, o_ref in zip(in_refs, out_refs):
        o_ref[...] = x_ref[...].astype(BF16)


def _rope_angles(n_tokens, dp):
    t = np.arange(n_tokens)
    inv = ROPE_BASE ** (-np.arange(0, dp, 2, dtype=np.float64) / dp)
    return (t // GRID_W)[:, None] * inv[None, :], (t % GRID_W)[:, None] * inv[None, :]


def _rope_tables(n_tokens, head_dim, identity):
    half = head_dim // 2
    if identity:
        return jnp.ones((n_tokens, head_dim), F32), jnp.zeros((n_tokens, head_dim), F32)
    row, col = _rope_angles(n_tokens, half)
    cos = np.concatenate([np.cos(row), np.cos(row), np.cos(col), np.cos(col)], axis=-1)
    sin = np.concatenate([-np.sin(row), np.sin(row), -np.sin(col), np.sin(col)], axis=-1)
    return jnp.asarray(cos, F32), jnp.asarray(sin, F32)


def _rope_apply(y, cos, sin, quarter):
    if 2 * quarter == V7X_LANES:
        return y * cos + pltpu.roll(y, quarter, 1) * sin
    lane = lax.broadcasted_iota(jnp.int32, y.shape, 1)
    first = (lane % (2 * quarter)) < quarter
    partner = jnp.where(first, pltpu.roll(y, V7X_LANES - quarter, 1), pltpu.roll(y, quarter, 1))
    return y * cos + partner * sin


def _proj_even_kernel(x_ref, sh_ref, sc_ref, xn_ref, shn_ref, scn_ref, w_ref, qg_ref, kg_ref, cos_ref, sin_ref,
                      f_ref, q_ref, k_ref, v_ref, u_scr, *, fdim, qdim, kvdim, hd):
    i = pl.program_id(0)
    slot = i % 2

    @pl.when(i == 0)
    def _():
        _stage_modulated(x_ref, sh_ref, sc_ref, u_scr, 0, 0, 1)

    u = u_scr[slot]
    cos = cos_ref[...]
    sin = sin_ref[...]

    def heads(lo, width, gain_ref, out_ref):
        seg = _tile(width, 512, hd)
        gain = gain_ref[...]
        for s in range(width // seg):
            y = jnp.dot(u, w_ref[:, lo + s * seg:lo + (s + 1) * seg], preferred_element_type=F32)
            for h in range(seg // hd):
                yh = y[:, h * hd:(h + 1) * hd]
                ms = jnp.mean(yh * yh, axis=-1, keepdims=True)
                yh = yh * lax.rsqrt(ms + EPS) * gain
                yh = _rope_apply(yh, cos, sin, hd // 4)
                out_ref[:, s * seg + h * hd:s * seg + (h + 1) * hd] = yh.astype(BF16)

    heads(fdim, qdim, qg_ref, q_ref)
    heads(fdim + qdim, kvdim, kg_ref, k_ref)
    f_ref[...] = jnp.dot(u, w_ref[:, :fdim], preferred_element_type=F32).astype(BF16)
    v_ref[...] = jnp.dot(u, w_ref[:, fdim + qdim + kvdim:], preferred_element_type=F32).astype(BF16)
    _stage_modulated(xn_ref, shn_ref, scn_ref, u_scr, 1 - slot, 0, 1)


def _first_tile_specs(tm, d, shift, scale):
    once = pl.Buffered(1)
    return [pl.BlockSpec((tm, d), lambda i, *_: (0, 0), pipeline_mode=once),
            pl.BlockSpec((None, 1, d), lambda i, *_: (0, 0, 0), pipeline_mode=once),
            pl.BlockSpec((None, 1, d), lambda i, *_: (0, 0, 0), pipeline_mode=once)]


def _proj_even(h, shift, scale, rows_per_mod, w, q_gain, k_gain, cos, sin, dims):
    n, d = h.shape
    fdim, qdim, kvdim = dims
    hd = ATTN_HEAD_DIM
    pos_rows = cos.shape[0]
    tm = _tile(math.gcd(rows_per_mod, pos_rows), 512)
    n_tiles = n // tm
    pos_tiles = pos_rows // tm
    kern = functools.partial(_proj_even_kernel, fdim=fdim, qdim=qdim, kvdim=kvdim, hd=hd)
    const = lambda a: pl.BlockSpec(a.shape, lambda i: (0,) * a.ndim)
    return pl.pallas_call(
        kern,
        grid=(n_tiles,),
        in_specs=_first_tile_specs(tm, d, shift, scale) + [
            _row_spec(tm, d, n_tiles),
            _mod_spec(shift, rows_per_mod, tm, n_tiles), _mod_spec(scale, rows_per_mod, tm, n_tiles),
            pl.BlockSpec((None,) + w.shape[1:], lambda i: (0, 0, 0)),
            const(q_gain), const(k_gain),
            pl.BlockSpec((tm, hd), lambda i: (i % pos_tiles, 0)),
            pl.BlockSpec((tm, hd), lambda i: (i % pos_tiles, 0)),
        ],
        out_specs=[_row_spec(tm, fdim), _row_spec(tm, qdim), _row_spec(tm, kvdim), _row_spec(tm, kvdim)],
        out_shape=[jax.ShapeDtypeStruct((n, fdim), BF16), jax.ShapeDtypeStruct((n, qdim), BF16),
                   jax.ShapeDtypeStruct((n, kvdim), BF16), jax.ShapeDtypeStruct((n, kvdim), BF16)],
        scratch_shapes=[pltpu.VMEM((2, tm, d), BF16)],
        compiler_params=_params("arbitrary"),
        name="proj_even",
    )(h, shift, scale, h, shift, scale, w, q_gain, k_gain, cos, sin)


def _attn_kernel(q_ref, *refs, n_seg, n_cast, group, hd, scale, n_split):
    k_refs, v_refs = refs[:n_seg], refs[n_seg:2 * n_seg]
    cast_in, refs = refs[2 * n_seg:2 * n_seg + n_cast], refs[2 * n_seg + n_cast:]
    o_ref, cast_out = refs[0], refs[1:1 + n_cast]
    s_scr, p_scr, v_scr = refs[1 + n_cast:]
    tq = q_ref.shape[0]
    _run_casts(cast_in, cast_out)

    @pl.when(pl.program_id(2) == 0)
    def _():
        off = 0
        for v in v_refs:
            n = v.shape[0]
            v_scr[off:off + n, :hd] = v[...]
            v_scr[off:off + n, hd:] = jnp.ones((n, hd), BF16)
            off += n

    q = q_ref[...]
    qs = jnp.concatenate([q[:, g * hd:(g + 1) * hd] for g in range(group)], axis=0)
    nt = (((1,), (1,)), ((), ()))
    rows = qs.shape[0] // n_split
    blocks = [slice(i * rows, (i + 1) * rows) for i in range(n_split)]
    for blk in blocks:
        off = 0
        for k in k_refs:
            n = k.shape[0]
            s_scr[blk, off:off + n] = lax.dot_general(qs[blk], k[...], nt, preferred_element_type=F32)
            off += n
    outs = []
    for blk in blocks:
        m = jnp.max(s_scr[blk, :], axis=-1, keepdims=True)
        p_scr[blk, :] = jnp.exp2((s_scr[blk, :] - m) * (scale * math.log2(math.e))).astype(BF16)
    for blk in blocks:
        o = jnp.dot(p_scr[blk, :], v_scr[...], preferred_element_type=F32)
        outs.append(o[:, :hd] / o[:, hd:])
    o = jnp.concatenate(outs, axis=0)
    for g in range(group):
        o_ref[:, g * hd:(g + 1) * hd] = o[g * tq:(g + 1) * tq].astype(BF16)


def _attention(q, segments, batch, casts=()):
    hd, kvh = ATTN_HEAD_DIM, ATTN_KV_HEADS
    n, qdim = q.shape
    group = qdim // (kvh * hd)
    t_q = n // batch
    tq = _tile(t_q, 512)
    nq = t_q // tq
    n_keys = sum(k.shape[0] for k, _ in segments) // batch
    grid = (batch, kvh, nq)
    cast_in, cast_out, cast_shapes = _cast_plan(casts, grid)
    kern = functools.partial(_attn_kernel, n_seg=len(segments), n_cast=len(casts), group=group, hd=hd,
                             scale=hd ** -0.5, n_split=4)
    kv_spec = lambda a: pl.BlockSpec((a.shape[0] // batch, hd), lambda b, kh, t: (b, kh))
    q_spec = pl.BlockSpec((tq, group * hd), lambda b, kh, t: (b * nq + t, kh))
    out = pl.pallas_call(
        kern,
        grid=grid,
        in_specs=[q_spec] + [kv_spec(k) for k, _ in segments] + [kv_spec(v) for _, v in segments] + cast_in,
        out_specs=[q_spec] + cast_out,
        out_shape=[jax.ShapeDtypeStruct((n, qdim), BF16)] + cast_shapes,
        scratch_shapes=[pltpu.VMEM((group * tq, n_keys), F32), pltpu.VMEM((group * tq, n_keys), BF16),
                        pltpu.VMEM((n_keys, 2 * hd), BF16)],
        compiler_params=_params("arbitrary", "arbitrary", "arbitrary"),
        name="attention",
    )(q, *[k for k, _ in segments], *[v for _, v in segments], *[w[0] for w in casts])
    return out[0], out[1:]


def _fourier_kernel(u_ref, cm_ref, cs_ref, o_ref, z_scr, *, groups, gd, scale):
    t = u_ref.shape[0]

    @pl.when(pl.program_id(1) == 0)
    def _():
        for g in range(groups):
            cols = slice(g * gd, (g + 1) * gd)
            z = jnp.dot(u_ref[:, cols], cm_ref[...], preferred_element_type=F32)
            z_scr[0:t, cols] = z[:, :gd].astype(BF16)
            z_scr[t:2 * t, cols] = z[:, gd:].astype(BF16)

    y = jnp.dot(cs_ref[...], z_scr[...], preferred_element_type=F32)
    o_ref[...] = (y * scale).astype(BF16)


def _dft_tables(t, gd):
    def angles(n):
        i = np.arange(n)
        return ((i[:, None] * i[None, :]) % n) * (2.0 * math.pi / n)
    a_c, a_t = angles(gd), angles(t)
    cm = np.concatenate([np.cos(a_c), -np.sin(a_c)], axis=1)
    cs = np.concatenate([np.cos(a_t), np.sin(a_t)], axis=1)
    return jnp.asarray(cm.astype(BF16)), jnp.asarray(cs.astype(BF16))


def _fourier(u, batch, tables):
    n, fdim = u.shape
    t = n // batch
    gd = fdim // FOURIER_GROUPS
    cm, cs = tables
    tr = _tile(t, 512)
    kern = functools.partial(_fourier_kernel, groups=FOURIER_GROUPS, gd=gd, scale=(t * gd) ** -0.5)
    return pl.pallas_call(
        kern,
        grid=(batch, t // tr),
        in_specs=[
            pl.BlockSpec((t, fdim), lambda b, r: (b, 0)),
            pl.BlockSpec(cm.shape, lambda b, r: (0, 0)),
            pl.BlockSpec((tr, 2 * t), lambda b, r: (r, 0)),
        ],
        out_specs=pl.BlockSpec((tr, fdim), lambda b, r: (b * (t // tr) + r, 0)),
        out_shape=jax.ShapeDtypeStruct((n, fdim), BF16),
        scratch_shapes=[pltpu.VMEM((2 * t, fdim), BF16)],
        compiler_params=_params("parallel", "arbitrary"),
        name="fourier",
    )(u, cm, cs)


def _proj_odd_kernel(x_ref, sh_ref, sc_ref, xn_ref, shn_ref, scn_ref, w_ref, ta_ref, tb_ref, o_ref, u_scr,
                     *, n_steps, hd):
    i, j = pl.program_id(0), pl.program_id(1)
    slot = i % 2

    @pl.when((i == 0) & (j == 0))
    def _():
        _stage_modulated(x_ref, sh_ref, sc_ref, u_scr, 0, 0, 1)

    u = u_scr[slot]
    for c0 in range(0, o_ref.shape[1], hd):
        y = jnp.dot(u, w_ref[:, c0:c0 + hd], preferred_element_type=F32)
        for l0 in range(0, hd, V7X_LANES):
            lanes = slice(l0, l0 + V7X_LANES)
            r = _rope_apply(y[:, lanes], ta_ref[:, lanes], tb_ref[:, lanes], hd // 4)
            o_ref[:, c0 + l0:c0 + l0 + V7X_LANES] = r.astype(BF16)
    _stage_modulated(xn_ref, shn_ref, scn_ref, u_scr, 1 - slot, j, n_steps)


def _proj_odd_tables(cos, sin, k_scale):
    a = jnp.stack([cos, cos * k_scale, jnp.ones_like(cos)])
    b = jnp.stack([sin, sin * k_scale, jnp.zeros_like(sin)])
    return a, b


def _proj_odd(h, shift, scale, rows_per_mod, w, tables, qk_dim):
    n, d = h.shape
    n_out = w.shape[2]
    ta, tb = tables
    pos_rows, hd = ta.shape[1:]
    tm = _tile(math.gcd(rows_per_mod, pos_rows), 1024)
    tn = _tile(math.gcd(qk_dim, n_out), 1024, hd)
    n_tiles = n // tm
    pos_tiles = pos_rows // tm
    q_tiles = qk_dim // tn
    table_spec = pl.BlockSpec((None, tm, hd), lambda i, j: (jnp.minimum(j // q_tiles, 2), i % pos_tiles, 0))
    return pl.pallas_call(
        functools.partial(_proj_odd_kernel, n_steps=n_out // tn, hd=hd),
        grid=(n_tiles, n_out // tn),
        in_specs=_first_tile_specs(tm, d, shift, scale) + [
            _row_spec(tm, d, n_tiles),
            _mod_spec(shift, rows_per_mod, tm, n_tiles), _mod_spec(scale, rows_per_mod, tm, n_tiles),
            pl.BlockSpec((None, d, tn), lambda i, j: (0, 0, j)),
            table_spec, table_spec,
        ],
        out_specs=pl.BlockSpec((tm, tn), lambda i, j: (i, j)),
        out_shape=jax.ShapeDtypeStruct((n, n_out), BF16),
        scratch_shapes=[pltpu.VMEM((2, tm, d), BF16)],
        compiler_params=_params("arbitrary", "arbitrary"),
        name="proj_odd",
    )(h, shift, scale, h, shift, scale, w, ta, tb)


def _retention_kernel(lgf_ref, lgb_ref, qc_ref, kc_ref, vc_ref, gc_ref, ql_ref, kl_ref, vl_ref, gl_ref,
                      *refs, chunk, n_cast):
    cast_in, refs = refs[:n_cast], refs[n_cast:]
    oc_ref, ol_ref = refs[:2]
    cast_out = refs[2:2 + n_cast]
    sf_scr, sb_scr, d_scr, dec_scr, partc_scr, partl_scr = refs[2 + n_cast:]
    _run_casts(cast_in, cast_out)
    c = chunk
    dk = sf_scr.shape[0]
    lgf = lgf_ref[...][:, :1]
    lgb = lgb_ref[...][:, :1]
    diff = (lax.broadcasted_iota(jnp.int32, (c, c), 0)
            - lax.broadcasted_iota(jnp.int32, (c, c), 1)).astype(F32)
    d_scr[...] = jnp.where(diff >= 0.0, jnp.exp(jnp.maximum(diff, 0.0) * lgf),
                           jnp.exp(jnp.maximum(-diff, 0.0) * lgb))
    idx = lax.broadcasted_iota(jnp.int32, (c, dk), 0).astype(F32)
    dec_scr[0] = jnp.exp((idx + 1.0) * lgf).astype(BF16)
    dec_scr[1] = jnp.exp((c - 1.0 - idx) * lgf).astype(BF16)
    dec_scr[2] = jnp.exp((c - idx) * lgb).astype(BF16)
    dec_scr[3] = jnp.exp(idx * lgb).astype(BF16)
    chunk_dec_f, chunk_dec_b = jnp.exp(c * lgf), jnp.exp(c * lgb)
    nt = (((1,), (1,)), ((), ()))
    tn = (((0,), (0,)), ((), ()))

    sf_scr[...] = jnp.zeros_like(sf_scr)
    sb_scr[...] = jnp.zeros_like(sb_scr)

    def forward(seg, i):
        q_ref, k_ref, v_ref = seg[:3]
        rows = slice(i * c, (i + 1) * c)
        q, k, v = q_ref[rows, :], k_ref[rows, :], v_ref[rows, :]
        s = lax.dot_general(q, k, nt, preferred_element_type=F32) * d_scr[...]
        lhs = jnp.concatenate([s.astype(BF16), q * dec_scr[0]], axis=1)
        rhs = jnp.concatenate([v, sf_scr[...].astype(BF16)], axis=0)
        o = jnp.dot(lhs, rhs, preferred_element_type=F32)
        sf_scr[...] = sf_scr[...] * chunk_dec_f + lax.dot_general(k * dec_scr[1], v, tn,
                                                                  preferred_element_type=F32)
        return o

    def backward(seg, i):
        q_ref, k_ref, v_ref = seg[:3]
        rows = slice(i * c, (i + 1) * c)
        q, k, v = q_ref[rows, :], k_ref[rows, :], v_ref[rows, :]
        o = jnp.dot(q * dec_scr[2], sb_scr[...].astype(BF16), preferred_element_type=F32)
        sb_scr[...] = sb_scr[...] * chunk_dec_b + lax.dot_general(k * dec_scr[3], v, tn,
                                                                  preferred_element_type=F32)
        return o

    def finish(seg, i, o):
        g_ref, o_ref = seg[3], seg[4]
        rows = slice(i * c, (i + 1) * c)
        o = o * lax.rsqrt(jnp.mean(o * o, axis=-1, keepdims=True) + EPS)
        o_ref[rows, :] = _silu(g_ref[rows, :]) * o.astype(BF16)

    def sweep(seg):
        part = seg[5]
        n = seg[0].shape[0] // c
        for t in range(n):
            i_f, i_b = t, n - 1 - t
            o_f, o_b = forward(seg, i_f), backward(seg, i_b)
            if i_f == i_b:
                finish(seg, i_f, o_f + o_b)
            elif i_f < i_b:
                part[i_f * c:(i_f + 1) * c, :] = o_f
                part[i_b * c:(i_b + 1) * c, :] = o_b
            else:
                finish(seg, i_f, o_f + part[i_f * c:(i_f + 1) * c, :])
                finish(seg, i_b, o_b + part[i_b * c:(i_b + 1) * c, :])

    sweep((qc_ref, kc_ref, vc_ref, gc_ref, oc_ref, partc_scr))
    sweep((ql_ref, kl_ref, vl_ref, gl_ref, ol_ref, partl_scr))


def _retention(p_ctx, p_lat, lg_fwd, lg_bwd, batch, casts=()):
    heads = lg_fwd.shape[0]
    n_out = p_lat.shape[1]
    dk = n_out // (6 * heads)
    dv = 2 * dk
    t_c, t_l = p_ctx.shape[0] // batch, p_lat.shape[0] // batch
    c = _tile(math.gcd(t_c, t_l), 256)
    lanes = lambda lg: jnp.broadcast_to(lg.astype(F32)[:, None, None], (heads, 1, V7X_LANES))
    lg_spec = pl.BlockSpec((None, 1, V7X_LANES), lambda b, h: (h, 0, 0))

    def specs(t):
        return [pl.BlockSpec((t, dk), lambda b, h: (b, h)),
                pl.BlockSpec((t, dk), lambda b, h: (b, heads + h)),
                pl.BlockSpec((t, dv), lambda b, h: (b, heads + h)),
                pl.BlockSpec((t, dv), lambda b, h: (b, 2 * heads + h))]

    out_spec = lambda t: pl.BlockSpec((t, dv), lambda b, h: (b, h))
    grid = (batch, heads)
    cast_in, cast_out, cast_shapes = _cast_plan(casts, grid)
    out = pl.pallas_call(
        functools.partial(_retention_kernel, chunk=c, n_cast=len(casts)),
        grid=grid,
        in_specs=[lg_spec, lg_spec] + specs(t_c) + specs(t_l) + cast_in,
        out_specs=[out_spec(t_c), out_spec(t_l)] + cast_out,
        out_shape=[jax.ShapeDtypeStruct((p_ctx.shape[0], heads * dv), BF16),
                   jax.ShapeDtypeStruct((p_lat.shape[0], heads * dv), BF16)] + cast_shapes,
        scratch_shapes=[pltpu.VMEM((dk, dv), F32), pltpu.VMEM((dk, dv), F32), pltpu.VMEM((c, c), F32),
                        pltpu.VMEM((4, c, dk), BF16),
                        pltpu.VMEM((t_c, dv), F32), pltpu.VMEM((t_l, dv), F32)],
        compiler_params=_params("arbitrary", "arbitrary"),
        name="retention",
    )(lanes(lg_fwd), lanes(lg_bwd), p_ctx, p_ctx, p_ctx, p_ctx, p_lat, p_lat, p_lat, p_lat,
      *[w[0] for w in casts])
    return out[0], out[1], out[2:]


def _outproj_kernel(*refs, n_parts, seg):
    a_refs, w_refs = refs[:n_parts], refs[n_parts:2 * n_parts]
    res_ref, gate_ref, o_ref = refs[2 * n_parts:]
    for c0 in range(0, o_ref.shape[1], seg):
        cols = slice(c0, c0 + seg)
        acc = functools.reduce(jnp.add, [jnp.dot(a[...], w[:, cols], preferred_element_type=F32)
                                         for a, w in zip(a_refs, w_refs)])
        o_ref[:, cols] = res_ref[:, cols] + gate_ref[:, cols] * acc


def _outproj(parts, weights, res, gate, rows_per_mod):
    n, d = res.shape
    tm = _tile(rows_per_mod, 512)
    return pl.pallas_call(
        functools.partial(_outproj_kernel, n_parts=len(parts), seg=_tile(d, 512, V7X_LANES)),
        grid=(n // tm,),
        in_specs=[_row_spec(tm, a.shape[1]) for a in parts]
        + [pl.BlockSpec((None, w.shape[1], d), lambda i: (0, 0, 0), pipeline_mode=pl.Buffered(1))
           for w in weights]
        + [_row_spec(tm, d), _mod_spec(gate, rows_per_mod, tm)],
        out_specs=_row_spec(tm, d),
        out_shape=jax.ShapeDtypeStruct((n, d), F32),
        compiler_params=_params("parallel"),
        name="outproj",
    )(*parts, *weights, res, gate)


def _ffn_kernel(x_ref, sh_ref, sc_ref, gate_ref, wg_ref, wu_ref, wo_ref, o_ref, u_scr, *, n_steps):
    j = pl.program_id(1)

    @pl.when(j == 0)
    def _():
        _modulate_rows(x_ref, sh_ref, sc_ref, u_scr)
        o_ref[...] = jnp.zeros_like(o_ref)

    u = u_scr[...]
    g = jnp.dot(u, wg_ref[...], preferred_element_type=F32)
    up = jnp.dot(u, wu_ref[...], preferred_element_type=F32)
    act = (_silu(g) * up).astype(BF16)
    o_ref[...] += jnp.dot(act, wo_ref[...], preferred_element_type=F32)

    @pl.when(j == n_steps - 1)
    def _():
        o_ref[...] = x_ref[...] + gate_ref[...] * o_ref[...]


def _ffn(h, shift, scale, gate, rows_per_mod, w_in, w_out):
    n, d = h.shape
    f = w_out.shape[1]
    tm = _tile(rows_per_mod, 1024)
    tf = _tile(f, 512, V7X_LANES)
    nf = f // tf
    n_tiles = n // tm
    return pl.pallas_call(
        functools.partial(_ffn_kernel, n_steps=nf),
        grid=(n_tiles, nf),
        in_specs=[
            _row_spec(tm, d),
            _mod_spec(shift, rows_per_mod, tm), _mod_spec(scale, rows_per_mod, tm),
            _mod_spec(gate, rows_per_mod, tm),
            pl.BlockSpec((None, d, tf), lambda i, j: (0, 0, j)),
            pl.BlockSpec((None, d, tf), lambda i, j: (0, 0, nf + j)),
            pl.BlockSpec((None, tf, d), lambda i, j: (0, j, 0)),
        ],
        out_specs=_row_spec(tm, d),
        out_shape=jax.ShapeDtypeStruct((n, d), F32),
        scratch_shapes=[pltpu.VMEM((tm, d), BF16)],
        compiler_params=_params("parallel", "arbitrary"),
        name="ffn",
    )(h, shift, scale, gate, w_in, w_in, w_out)


def _even_layer(h_ctx, h_lat, mod_ctx, mod_lat, weights, q_gain, k_gain, batch, need_ctx, tables, casts):
    n_ctx = h_ctx.shape[0]
    n_lat = h_lat.shape[0]
    hd = ATTN_HEAD_DIM
    w_in = weights[0]
    q_gain, k_gain = q_gain.reshape(1, hd), k_gain.reshape(1, hd)
    kvdim = ATTN_KV_HEADS * hd
    fdim = FOURIER_GROUPS * (h_lat.shape[1] // 16)
    dims = (fdim, w_in.shape[2] - fdim - 2 * kvdim, kvdim)

    f_c, q_c, k_c, v_c = _proj_even(h_ctx, mod_ctx[0], mod_ctx[1], n_ctx, w_in, q_gain, k_gain,
                                    *tables["rope_even_ctx"], dims)
    f_l, q_l, k_l, v_l = _proj_even(h_lat, mod_lat[0], mod_lat[1], n_lat // batch, w_in, q_gain, k_gain,
                                    *tables["rope_even_lat"], dims)
    a_l, cast_out = _attention(q_l, [(k_l, v_l), (k_c, v_c)], batch, casts)
    weights = list(weights) + list(cast_out)
    w_f, w_a, ffn_weights, next_weights = weights[1], weights[2], weights[3:5], weights[5:]
    m_l = _fourier(f_l, batch, tables["dft_lat"])
    h_lat = _outproj([m_l, a_l], [w_f, w_a], h_lat, mod_lat[2], n_lat // batch)
    if need_ctx:
        a_c, _ = _attention(q_c, [(k_c, v_c)], batch)
        m_c = _fourier(f_c, batch, tables["dft_ctx"])
        h_ctx = _outproj([m_c, a_c], [w_f, w_a], h_ctx, mod_ctx[2], n_ctx)
    return h_ctx, h_lat, ffn_weights, next_weights


def _odd_layer(h_ctx, h_lat, mod_ctx, mod_lat, weights, lg_fwd, lg_bwd, batch, need_ctx, tables, casts):
    n_ctx = h_ctx.shape[0]
    n_lat = h_lat.shape[0]
    w_in = weights[0]
    qk_dim = w_in.shape[2] // 6
    p_c = _proj_odd(h_ctx, mod_ctx[0], mod_ctx[1], n_ctx, w_in, tables["rope_odd_ctx"], qk_dim)
    p_l = _proj_odd(h_lat, mod_lat[0], mod_lat[1], n_lat // batch, w_in, tables["rope_odd_lat"], qk_dim)
    m_c, m_l, cast_out = _retention(p_c, p_l, lg_fwd, lg_bwd, batch, casts)
    weights = list(weights) + list(cast_out)
    w_out, ffn_weights, next_weights = weights[1], weights[2:4], weights[4:]
    h_lat = _outproj([m_l], [w_out], h_lat, mod_lat[2], n_lat // batch)
    if need_ctx:
        h_ctx = _outproj([m_c], [w_out], h_ctx, mod_ctx[2], n_ctx)
    return h_ctx, h_lat, ffn_weights, next_weights


def kernel(x, c, ctx, c_ctx, w_mod, b_mod, w_in_even, w_out_even, q_gain_even, k_gain_even,
           w_in_odd, w_out_odd, log_decay_fwd, log_decay_bwd, w_ffn_in, w_ffn_out):
    batch, seq, d = x.shape
    ctx_len = ctx.shape[1]
    depth = w_mod.shape[0]
    n_lat, n_ctx = batch * seq, batch * ctx_len
    h_lat, h_ctx = x.reshape(n_lat, d), ctx.reshape(n_ctx, d)

    rows = -(-(batch + 1) // 8) * 8
    c_all = jnp.concatenate([c, c_ctx[None], jnp.zeros((rows - batch - 1, d), F32)], axis=0)
    mods = _modulation(c_all, w_mod, b_mod).reshape(depth, rows, N_MOD, d)

    fdim = FOURIER_GROUPS * (d // 16)

    def weight_slices(i):
        j = i // 2
        if i % 2 == 0:
            mixer = [(w_in_even, j, 0, d), (w_out_even, j, 0, fdim), (w_out_even, j, fdim, w_out_even.shape[1])]
        else:
            mixer = [(w_in_odd, j, 0, d), (w_out_odd, j, 0, w_out_odd.shape[1])]
        return mixer + [(w_ffn_in, i, 0, d), (w_ffn_out, i, 0, w_ffn_out.shape[1])]

    ret_hd = w_in_odd.shape[2] // (6 * log_decay_fwd.shape[1])
    ctx_rows = _tile(n_ctx, 1024)
    tables = {
        "rope_even_lat": _rope_tables(seq, ATTN_HEAD_DIM, False),
        "rope_even_ctx": _rope_tables(ctx_rows, ATTN_HEAD_DIM, True),
        "rope_odd_lat": _proj_odd_tables(*_rope_tables(seq, ret_hd, False), ret_hd ** -0.5),
        "rope_odd_ctx": _proj_odd_tables(*_rope_tables(ctx_rows, ret_hd, True), ret_hd ** -0.5),
        "dft_lat": _dft_tables(seq, d // 16),
        "dft_ctx": _dft_tables(ctx_len, d // 16),
    }

    first = weight_slices(0)
    weights, late = [_to_bf16(*first[0])], first[1:]
    for i in range(depth):
        need_ctx = i < depth - 1
        casts = late + (weight_slices(i + 1) if i + 1 < depth else [])
        mod_lat = [mods[i, :batch, k][:, None, :] for k in range(N_MOD)]
        mod_ctx = [mods[i, batch:batch + 1, k][:, None, :] for k in range(N_MOD)]
        j = i // 2
        if i % 2 == 0:
            h_ctx, h_lat, ffn_weights, weights = _even_layer(
                h_ctx, h_lat, mod_ctx, mod_lat, weights, q_gain_even[j], k_gain_even[j], batch, need_ctx,
                tables, casts)
        else:
            h_ctx, h_lat, ffn_weights, weights = _odd_layer(
                h_ctx, h_lat, mod_ctx, mod_lat, weights, log_decay_fwd[j], log_decay_bwd[j], batch, need_ctx,
                tables, casts)
        late = []
        h_lat = _ffn(h_lat, mod_lat[3], mod_lat[4], mod_lat[5], n_lat // batch, *ffn_weights)
        if need_ctx:
            h_ctx = _ffn(h_ctx, mod_ctx[3], mod_ctx[4], mod_ctx[5], n_ctx, *ffn_weights)
    return h_lat.reshape(batch, seq, d)
```

```python
import functools
import math

import jax
import jax.numpy as jnp
import numpy as np
from jax import lax
from jax.experimental import pallas as pl
from jax.experimental.pallas import tpu as pltpu

F32 = jnp.float32
BF16 = jnp.bfloat16

EPS = 1e-6
ROPE_BASE = 10000.0
GRID_W = 64
N_MOD = 6
FOURIER_GROUPS = 4
ATTN_HEAD_DIM = 128
ATTN_KV_HEADS = 4

V7X_LANES = 128
V7X_VMEM_BYTES = 64 * 1024 * 1024
VMEM_LIMIT_BYTES = V7X_VMEM_BYTES - 8 * 1024 * 1024


def _tile(n, target, align=8):
    best = None
    for t in range(align, min(n, target) + 1, align):
        if n % t == 0:
            best = t
    return n if best is None else best


def _params(*semantics):
    return pltpu.CompilerParams(dimension_semantics=semantics, vmem_limit_bytes=VMEM_LIMIT_BYTES)


def _silu(x):
    return x * jax.nn.sigmoid(x)


def _norm_mod(x, shift, scale):
    ms = jnp.mean(x * x, axis=-1, keepdims=True)
    return x * lax.rsqrt(ms + EPS) * (1.0 + scale) + shift


def _next_tile(i, n_tiles):
    return jnp.minimum(i + 1, n_tiles - 1)


def _row_spec(tm, width, n_tiles=None):
    if n_tiles is None:
        return pl.BlockSpec((tm, width), lambda i, *_: (i, 0))
    return pl.BlockSpec((tm, width), lambda i, *_: (_next_tile(i, n_tiles), 0))


def _mod_spec(mod, rows_per_mod, tm, n_tiles=None):
    tiles = rows_per_mod // tm
    d = mod.shape[-1]
    if n_tiles is None:
        return pl.BlockSpec((None, 1, d), lambda i, *_: (i // tiles, 0, 0))
    return pl.BlockSpec((None, 1, d), lambda i, *_: (_next_tile(i, n_tiles) // tiles, 0, 0))


ROW_CHUNK = 16
LANE_CHUNK = 1024


def _modulate_rows(x_ref, sh_ref, sc_ref, dst, start=0, n_rows=None):
    n_rows = x_ref.shape[0] if n_rows is None else n_rows
    d = x_ref.shape[1]
    for r0 in range(0, n_rows, ROW_CHUNK):
        first = start + r0
        rows = pl.ds(first if isinstance(first, int) else pl.multiple_of(first, ROW_CHUNK), ROW_CHUNK)
        x = x_ref[rows, :]
        inv = lax.rsqrt(jnp.mean(x * x, axis=-1, keepdims=True) + EPS)
        for c0 in range(0, d, LANE_CHUNK):
            cols = slice(c0, min(c0 + LANE_CHUNK, d))
            y = x_ref[rows, cols] * inv * (1.0 + sc_ref[:, cols]) + sh_ref[:, cols]
            dst[rows, cols] = y.astype(BF16)


def _stage_modulated(x_ref, sh_ref, sc_ref, u_scr, slot, step, n_steps):
    tm = x_ref.shape[0]
    if n_steps == 1:
        _modulate_rows(x_ref, sh_ref, sc_ref, u_scr.at[slot])
    else:
        r = min(tm, -(-(-(-tm // n_steps)) // ROW_CHUNK) * ROW_CHUNK)
        start = pl.multiple_of(jnp.minimum(step * r, tm - r), ROW_CHUNK)
        _modulate_rows(x_ref, sh_ref, sc_ref, u_scr.at[slot], start, r)


def _mod_kernel(c_ref, w_ref, b_ref, o_ref):
    cond = _silu(c_ref[...]).astype(BF16)
    o_ref[...] = jnp.dot(cond, w_ref[...].astype(BF16), preferred_element_type=F32) + b_ref[...]


def _modulation(c_all, w_mod, b_mod):
    depth, d, n = w_mod.shape
    r = c_all.shape[0]
    tn = _tile(n, 1024, V7X_LANES)
    return pl.pallas_call(
        _mod_kernel,
        grid=(depth, n // tn),
        in_specs=[
            pl.BlockSpec((r, d), lambda i, j: (0, 0)),
            pl.BlockSpec((None, d, tn), lambda i, j: (i, 0, j)),
            pl.BlockSpec((None, 1, tn), lambda i, j: (i, 0, j)),
        ],
        out_specs=pl.BlockSpec((None, r, tn), lambda i, j: (i, 0, j)),
        out_shape=jax.ShapeDtypeStruct((depth, r, n), F32),
        compiler_params=_params("parallel", "parallel"),
        name="modulation",
    )(c_all, w_mod, b_mod.reshape(depth, 1, n))


def _reorder_head_dims(x, hd):
    half = V7X_LANES // 2
    if hd == 2 * V7X_LANES:
        t1, t2 = x[:, :V7X_LANES], x[:, V7X_LANES:]
        low = lax.broadcasted_iota(jnp.int32, t1.shape, 1) < half
        return jnp.concatenate([jnp.where(low, t1, pltpu.roll(t2, half, 1)),
                                jnp.where(low, pltpu.roll(t1, half, 1), t2)], axis=1)
    assert hd == V7X_LANES, hd
    lane = lax.broadcasted_iota(jnp.int32, x.shape, 1)
    quarter = hd // 4
    return jnp.where((lane >= quarter) & (lane < 2 * quarter), pltpu.roll(x, hd - quarter, 1),
                     jnp.where((lane >= 2 * quarter) & (lane < 3 * quarter), pltpu.roll(x, quarter, 1), x))


def _cast_block(x_ref, o_ref, reorder):
    if reorder is None:
        o_ref[...] = x_ref[...].astype(BF16)
        return
    hd, lo, hi = reorder
    n = x_ref.shape[1]
    if lo > 0:
        o_ref[:, :lo] = x_ref[:, :lo].astype(BF16)
    for h0 in range(lo, hi, hd):
        o_ref[:, h0:h0 + hd] = _reorder_head_dims(x_ref[:, h0:h0 + hd], hd).astype(BF16)
    if hi < n:
        o_ref[:, hi:] = x_ref[:, hi:].astype(BF16)


def _cast_kernel(x_ref, o_ref, *, reorder):
    _cast_block(x_ref, o_ref, reorder)


def _to_bf16(w, layer, row_lo, row_hi, reorder=None):
    n = w.shape[2]
    rows = row_hi - row_lo
    tn = n if reorder is not None else _tile(n, 4096, V7X_LANES)
    tk = _tile(math.gcd(rows, row_lo), max(16, 256 * 4096 // tn), 16)
    first = row_lo // tk
    return pl.pallas_call(
        functools.partial(_cast_kernel, reorder=reorder),
        grid=(rows // tk, n // tn),
        in_specs=[pl.BlockSpec((None, tk, tn), lambda r, c: (layer, first + r, c))],
        out_specs=pl.BlockSpec((None, tk, tn), lambda r, c: (0, r, c)),
        out_shape=jax.ShapeDtypeStruct((1, rows, n), BF16),
        compiler_params=_params("parallel", "parallel"),
        name="to_bf16",
    )(w)


def _cast_plan(casts, grid):
    steps = math.prod(grid)

    def step_of(*idx):
        s = 0
        for i, g in zip(idx, grid):
            s = s * g + i
        return s

    in_specs, out_specs, out_shapes = [], [], []
    for w, layer, lo, hi, _ in casts:
        rows, n = hi - lo, w.shape[2]
        unit = math.gcd(rows, lo)
        rb = min(t for t in range(16, unit + 1, 16) if unit % t == 0 and rows // t <= steps)
        nb, first = rows // rb, lo // rb
        in_specs.append(pl.BlockSpec(
            (None, rb, n),
            lambda *idx, layer=layer, first=first, nb=nb: (layer, first + jnp.minimum(step_of(*idx), nb - 1), 0)))
        out_specs.append(pl.BlockSpec(
            (None, rb, n), lambda *idx, nb=nb: (0, jnp.minimum(step_of(*idx), nb - 1), 0)))
        out_shapes.append(jax.ShapeDtypeStruct((1, rows, n), BF16))
    return in_specs, out_specs, out_shapes


def _run_casts(reorders, in_refs, out_refs):
    for reorder, x_ref, o_ref in zip(reorders, in_refs, out_refs):
        _cast_block(x_ref, o_ref, reorder)


def _rope_angles(n_tokens, dp):
    t = np.arange(n_tokens)
    inv = ROPE_BASE ** (-np.arange(0, dp, 2, dtype=np.float64) / dp)
    return (t // GRID_W)[:, None] * inv[None, :], (t % GRID_W)[:, None] * inv[None, :]


def _rope_tables(n_tokens, head_dim, identity):
    if identity:
        return jnp.ones((n_tokens, head_dim), F32), jnp.zeros((n_tokens, head_dim), F32)
    row, col = _rope_angles(n_tokens, head_dim // 2)
    cos = np.concatenate([np.cos(row), np.cos(col), np.cos(row), np.cos(col)], axis=-1)
    sin = np.concatenate([-np.sin(row), -np.sin(col), np.sin(row), np.sin(col)], axis=-1)
    return jnp.asarray(cos, F32), jnp.asarray(sin, F32)


def _reorder_gain(gain):
    g = gain.reshape(4, -1)
    return jnp.concatenate([g[0], g[2], g[1], g[3]])


def _proj_even_kernel(x_ref, sh_ref, sc_ref, xn_ref, shn_ref, scn_ref, w_ref, qg_ref, kg_ref, cos_ref, sin_ref,
                      f_ref, q_ref, k_ref, v_ref, u_scr, *, fdim, qdim, kvdim, hd):
    i = pl.program_id(0)
    slot = i % 2

    @pl.when(i == 0)
    def _():
        _stage_modulated(x_ref, sh_ref, sc_ref, u_scr, 0, 0, 1)

    u = u_scr[slot]
    cos = cos_ref[...]
    sin = sin_ref[...]

    def heads(lo, width, gain_ref, out_ref):
        seg = _tile(width, 512, hd)
        gain = gain_ref[...]
        for s in range(width // seg):
            y = jnp.dot(u, w_ref[:, lo + s * seg:lo + (s + 1) * seg], preferred_element_type=F32)
            for h in range(seg // hd):
                yh = y[:, h * hd:(h + 1) * hd]
                ms = jnp.mean(yh * yh, axis=-1, keepdims=True)
                yh = yh * lax.rsqrt(ms + EPS) * gain
                yh = yh * cos + pltpu.roll(yh, hd // 2, 1) * sin
                out_ref[:, s * seg + h * hd:s * seg + (h + 1) * hd] = yh.astype(BF16)

    heads(fdim, qdim, qg_ref, q_ref)
    heads(fdim + qdim, kvdim, kg_ref, k_ref)
    f_ref[...] = jnp.dot(u, w_ref[:, :fdim], preferred_element_type=F32).astype(BF16)
    v_ref[...] = jnp.dot(u, w_ref[:, fdim + qdim + kvdim:], preferred_element_type=F32).astype(BF16)
    _stage_modulated(xn_ref, shn_ref, scn_ref, u_scr, 1 - slot, 0, 1)


def _first_tile_specs(tm, d, shift, scale):
    once = pl.Buffered(1)
    return [pl.BlockSpec((tm, d), lambda i, *_: (0, 0), pipeline_mode=once),
            pl.BlockSpec((None, 1, d), lambda i, *_: (0, 0, 0), pipeline_mode=once),
            pl.BlockSpec((None, 1, d), lambda i, *_: (0, 0, 0), pipeline_mode=once)]


def _proj_even(h, shift, scale, rows_per_mod, w, q_gain, k_gain, cos, sin, dims):
    n, d = h.shape
    fdim, qdim, kvdim = dims
    hd = ATTN_HEAD_DIM
    pos_rows = cos.shape[0]
    tm = _tile(math.gcd(rows_per_mod, pos_rows), 512)
    n_tiles = n // tm
    pos_tiles = pos_rows // tm
    kern = functools.partial(_proj_even_kernel, fdim=fdim, qdim=qdim, kvdim=kvdim, hd=hd)
    const = lambda a: pl.BlockSpec(a.shape, lambda i: (0,) * a.ndim)
    return pl.pallas_call(
        kern,
        grid=(n_tiles,),
        in_specs=_first_tile_specs(tm, d, shift, scale) + [
            _row_spec(tm, d, n_tiles),
            _mod_spec(shift, rows_per_mod, tm, n_tiles), _mod_spec(scale, rows_per_mod, tm, n_tiles),
            pl.BlockSpec((None,) + w.shape[1:], lambda i: (0, 0, 0)),
            const(q_gain), const(k_gain),
            pl.BlockSpec((tm, hd), lambda i: (i % pos_tiles, 0)),
            pl.BlockSpec((tm, hd), lambda i: (i % pos_tiles, 0)),
        ],
        out_specs=[_row_spec(tm, fdim), _row_spec(tm, qdim), _row_spec(tm, kvdim), _row_spec(tm, kvdim)],
        out_shape=[jax.ShapeDtypeStruct((n, fdim), BF16), jax.ShapeDtypeStruct((n, qdim), BF16),
                   jax.ShapeDtypeStruct((n, kvdim), BF16), jax.ShapeDtypeStruct((n, kvdim), BF16)],
        scratch_shapes=[pltpu.VMEM((2, tm, d), BF16)],
        compiler_params=_params("arbitrary"),
        name="proj_even",
    )(h, shift, scale, h, shift, scale, w, q_gain, k_gain, cos, sin)


def _attn_kernel(q_ref, *refs, n_seg, reorders, group, hd, scale, n_split):
    n_cast = len(reorders)
    k_refs, v_refs = refs[:n_seg], refs[n_seg:2 * n_seg]
    cast_in, refs = refs[2 * n_seg:2 * n_seg + n_cast], refs[2 * n_seg + n_cast:]
    o_ref, cast_out = refs[0], refs[1:1 + n_cast]
    s_scr, p_scr, v_scr = refs[1 + n_cast:]
    tq = q_ref.shape[0]
    _run_casts(reorders, cast_in, cast_out)

    @pl.when(pl.program_id(2) == 0)
    def _():
        off = 0
        for v in v_refs:
            n = v.shape[0]
            v_scr[off:off + n, :hd] = v[...]
            v_scr[off:off + n, hd:] = jnp.ones((n, hd), BF16)
            off += n

    q = q_ref[...]
    qs = jnp.concatenate([q[:, g * hd:(g + 1) * hd] for g in range(group)], axis=0)
    nt = (((1,), (1,)), ((), ()))
    rows = qs.shape[0] // n_split
    blocks = [slice(i * rows, (i + 1) * rows) for i in range(n_split)]
    for blk in blocks:
        off = 0
        for k in k_refs:
            n = k.shape[0]
            s_scr[blk, off:off + n] = lax.dot_general(qs[blk], k[...], nt, preferred_element_type=F32)
            off += n
    outs = []
    for blk in blocks:
        m = jnp.max(s_scr[blk, :], axis=-1, keepdims=True)
        p_scr[blk, :] = jnp.exp2((s_scr[blk, :] - m) * (scale * math.log2(math.e))).astype(BF16)
    for blk in blocks:
        o = jnp.dot(p_scr[blk, :], v_scr[...], preferred_element_type=F32)
        outs.append(o[:, :hd] / o[:, hd:])
    o = jnp.concatenate(outs, axis=0)
    for g in range(group):
        o_ref[:, g * hd:(g + 1) * hd] = o[g * tq:(g + 1) * tq].astype(BF16)


def _attention(q, segments, batch, casts=()):
    hd, kvh = ATTN_HEAD_DIM, ATTN_KV_HEADS
    n, qdim = q.shape
    group = qdim // (kvh * hd)
    t_q = n // batch
    tq = _tile(t_q, 512)
    nq = t_q // tq
    n_keys = sum(k.shape[0] for k, _ in segments) // batch
    grid = (batch, kvh, nq)
    cast_in, cast_out, cast_shapes = _cast_plan(casts, grid)
    kern = functools.partial(_attn_kernel, n_seg=len(segments), reorders=tuple(w[4] for w in casts), group=group, hd=hd,
                             scale=hd ** -0.5, n_split=4)
    kv_spec = lambda a: pl.BlockSpec((a.shape[0] // batch, hd), lambda b, kh, t: (b, kh))
    q_spec = pl.BlockSpec((tq, group * hd), lambda b, kh, t: (b * nq + t, kh))
    out = pl.pallas_call(
        kern,
        grid=grid,
        in_specs=[q_spec] + [kv_spec(k) for k, _ in segments] + [kv_spec(v) for _, v in segments] + cast_in,
        out_specs=[q_spec] + cast_out,
        out_shape=[jax.ShapeDtypeStruct((n, qdim), BF16)] + cast_shapes,
        scratch_shapes=[pltpu.VMEM((group * tq, n_keys), F32), pltpu.VMEM((group * tq, n_keys), BF16),
                        pltpu.VMEM((n_keys, 2 * hd), BF16)],
        compiler_params=_params("arbitrary", "arbitrary", "arbitrary"),
        name="attention",
    )(q, *[k for k, _ in segments], *[v for _, v in segments], *[w[0] for w in casts])
    return out[0], out[1:]


def _fourier_kernel(u_ref, cm_ref, cs_ref, o_ref, z_scr, *, groups, gd, scale):
    t = u_ref.shape[0]

    @pl.when(pl.program_id(1) == 0)
    def _():
        for g in range(groups):
            cols = slice(g * gd, (g + 1) * gd)
            z = jnp.dot(u_ref[:, cols], cm_ref[...], preferred_element_type=F32)
            z_scr[0:t, cols] = z[:, :gd].astype(BF16)
            z_scr[t:2 * t, cols] = z[:, gd:].astype(BF16)

    y = jnp.dot(cs_ref[...], z_scr[...], preferred_element_type=F32)
    o_ref[...] = (y * scale).astype(BF16)


def _dft_tables(t, gd):
    def angles(n):
        i = np.arange(n)
        return ((i[:, None] * i[None, :]) % n) * (2.0 * math.pi / n)
    a_c, a_t = angles(gd), angles(t)
    cm = np.concatenate([np.cos(a_c), -np.sin(a_c)], axis=1)
    cs = np.concatenate([np.cos(a_t), np.sin(a_t)], axis=1)
    return jnp.asarray(cm.astype(BF16)), jnp.asarray(cs.astype(BF16))


def _fourier(u, batch, tables):
    n, fdim = u.shape
    t = n // batch
    gd = fdim // FOURIER_GROUPS
    cm, cs = tables
    tr = _tile(t, 512)
    kern = functools.partial(_fourier_kernel, groups=FOURIER_GROUPS, gd=gd, scale=(t * gd) ** -0.5)
    return pl.pallas_call(
        kern,
        grid=(batch, t // tr),
        in_specs=[
            pl.BlockSpec((t, fdim), lambda b, r: (b, 0)),
            pl.BlockSpec(cm.shape, lambda b, r: (0, 0)),
            pl.BlockSpec((tr, 2 * t), lambda b, r: (r, 0)),
        ],
        out_specs=pl.BlockSpec((tr, fdim), lambda b, r: (b * (t // tr) + r, 0)),
        out_shape=jax.ShapeDtypeStruct((n, fdim), BF16),
        scratch_shapes=[pltpu.VMEM((2 * t, fdim), BF16)],
        compiler_params=_params("parallel", "arbitrary"),
        name="fourier",
    )(u, cm, cs)


def _proj_odd_kernel(x_ref, sh_ref, sc_ref, xn_ref, shn_ref, scn_ref, w_ref, ta_ref, tb_ref, o_ref, u_scr,
                     *, n_steps, hd):
    i, j = pl.program_id(0), pl.program_id(1)
    slot = i % 2

    @pl.when((i == 0) & (j == 0))
    def _():
        _stage_modulated(x_ref, sh_ref, sc_ref, u_scr, 0, 0, 1)

    u = u_scr[slot]
    half = hd // 2
    for c0 in range(0, o_ref.shape[1], hd):
        y = jnp.dot(u, w_ref[:, c0:c0 + hd], preferred_element_type=F32)
        t1, t2 = y[:, :half], y[:, half:]
        o_ref[:, c0:c0 + half] = (t1 * ta_ref[:, :half] + t2 * tb_ref[:, :half]).astype(BF16)
        o_ref[:, c0 + half:c0 + hd] = (t2 * ta_ref[:, half:] + t1 * tb_ref[:, half:]).astype(BF16)
    _stage_modulated(xn_ref, shn_ref, scn_ref, u_scr, 1 - slot, j, n_steps)


def _proj_odd_tables(cos, sin, k_scale):
    a = jnp.stack([cos, cos * k_scale, jnp.ones_like(cos)])
    b = jnp.stack([sin, sin * k_scale, jnp.zeros_like(sin)])
    return a, b


def _proj_odd(h, shift, scale, rows_per_mod, w, tables, qk_dim):
    n, d = h.shape
    n_out = w.shape[2]
    ta, tb = tables
    pos_rows, hd = ta.shape[1:]
    tm = _tile(math.gcd(rows_per_mod, pos_rows), 1024)
    tn = _tile(math.gcd(qk_dim, n_out), 1024, hd)
    n_tiles = n // tm
    pos_tiles = pos_rows // tm
    q_tiles = qk_dim // tn
    table_spec = pl.BlockSpec((None, tm, hd), lambda i, j: (jnp.minimum(j // q_tiles, 2), i % pos_tiles, 0))
    return pl.pallas_call(
        functools.partial(_proj_odd_kernel, n_steps=n_out // tn, hd=hd),
        grid=(n_tiles, n_out // tn),
        in_specs=_first_tile_specs(tm, d, shift, scale) + [
            _row_spec(tm, d, n_tiles),
            _mod_spec(shift, rows_per_mod, tm, n_tiles), _mod_spec(scale, rows_per_mod, tm, n_tiles),
            pl.BlockSpec((None, d, tn), lambda i, j: (0, 0, j)),
            table_spec, table_spec,
        ],
        out_specs=pl.BlockSpec((tm, tn), lambda i, j: (i, j)),
        out_shape=jax.ShapeDtypeStruct((n, n_out), BF16),
        scratch_shapes=[pltpu.VMEM((2, tm, d), BF16)],
        compiler_params=_params("arbitrary", "arbitrary"),
        name="proj_odd",
    )(h, shift, scale, h, shift, scale, w, ta, tb)


def _retention_kernel(lgf_ref, lgb_ref, qc_ref, kc_ref, vc_ref, gc_ref, ql_ref, kl_ref, vl_ref, gl_ref,
                      *refs, chunk, reorders):
    n_cast = len(reorders)
    cast_in, refs = refs[:n_cast], refs[n_cast:]
    oc_ref, ol_ref = refs[:2]
    cast_out = refs[2:2 + n_cast]
    sf_scr, sb_scr, d_scr, dec_scr, partc_scr, partl_scr = refs[2 + n_cast:]
    _run_casts(reorders, cast_in, cast_out)
    c = chunk
    dk = sf_scr.shape[0]
    lgf = lgf_ref[...][:, :1]
    lgb = lgb_ref[...][:, :1]
    diff = (lax.broadcasted_iota(jnp.int32, (c, c), 0)
            - lax.broadcasted_iota(jnp.int32, (c, c), 1)).astype(F32)
    d_scr[...] = jnp.where(diff >= 0.0, jnp.exp(jnp.maximum(diff, 0.0) * lgf),
                           jnp.exp(jnp.maximum(-diff, 0.0) * lgb))
    idx = lax.broadcasted_iota(jnp.int32, (c, dk), 0).astype(F32)
    dec_scr[0] = jnp.exp((idx + 1.0) * lgf).astype(BF16)
    dec_scr[1] = jnp.exp((c - 1.0 - idx) * lgf).astype(BF16)
    dec_scr[2] = jnp.exp((c - idx) * lgb).astype(BF16)
    dec_scr[3] = jnp.exp(idx * lgb).astype(BF16)
    chunk_dec_f, chunk_dec_b = jnp.exp(c * lgf), jnp.exp(c * lgb)
    nt = (((1,), (1,)), ((), ()))
    tn = (((0,), (0,)), ((), ()))

    sf_scr[...] = jnp.zeros_like(sf_scr)
    sb_scr[...] = jnp.zeros_like(sb_scr)

    def forward(seg, i):
        q_ref, k_ref, v_ref = seg[:3]
        rows = slice(i * c, (i + 1) * c)
        q, k, v = q_ref[rows, :], k_ref[rows, :], v_ref[rows, :]
        s = lax.dot_general(q, k, nt, preferred_element_type=F32) * d_scr[...]
        lhs = jnp.concatenate([s.astype(BF16), q * dec_scr[0]], axis=1)
        rhs = jnp.concatenate([v, sf_scr[...].astype(BF16)], axis=0)
        o = jnp.dot(lhs, rhs, preferred_element_type=F32)
        sf_scr[...] = sf_scr[...] * chunk_dec_f + lax.dot_general(k * dec_scr[1], v, tn,
                                                                  preferred_element_type=F32)
        return o

    def backward(seg, i):
        q_ref, k_ref, v_ref = seg[:3]
        rows = slice(i * c, (i + 1) * c)
        q, k, v = q_ref[rows, :], k_ref[rows, :], v_ref[rows, :]
        o = jnp.dot(q * dec_scr[2], sb_scr[...].astype(BF16), preferred_element_type=F32)
        sb_scr[...] = sb_scr[...] * chunk_dec_b + lax.dot_general(k * dec_scr[3], v, tn,
                                                                  preferred_element_type=F32)
        return o

    def finish(seg, i, o):
        g_ref, o_ref = seg[3], seg[4]
        rows = slice(i * c, (i + 1) * c)
        o = o * lax.rsqrt(jnp.mean(o * o, axis=-1, keepdims=True) + EPS)
        o_ref[rows, :] = _silu(g_ref[rows, :]) * o.astype(BF16)

    def sweep(seg):
        part = seg[5]
        n = seg[0].shape[0] // c
        for t in range(n):
            i_f, i_b = t, n - 1 - t
            o_f, o_b = forward(seg, i_f), backward(seg, i_b)
            if i_f == i_b:
                finish(seg, i_f, o_f + o_b)
            elif i_f < i_b:
                part[i_f * c:(i_f + 1) * c, :] = o_f
                part[i_b * c:(i_b + 1) * c, :] = o_b
            else:
                finish(seg, i_f, o_f + part[i_f * c:(i_f + 1) * c, :])
                finish(seg, i_b, o_b + part[i_b * c:(i_b + 1) * c, :])

    sweep((qc_ref, kc_ref, vc_ref, gc_ref, oc_ref, partc_scr))
    sweep((ql_ref, kl_ref, vl_ref, gl_ref, ol_ref, partl_scr))


def _retention(p_ctx, p_lat, lg_fwd, lg_bwd, batch, casts=()):
    heads = lg_fwd.shape[0]
    n_out = p_lat.shape[1]
    dk = n_out // (6 * heads)
    dv = 2 * dk
    t_c, t_l = p_ctx.shape[0] // batch, p_lat.shape[0] // batch
    c = _tile(math.gcd(t_c, t_l), 256)
    lanes = lambda lg: jnp.broadcast_to(lg.astype(F32)[:, None, None], (heads, 1, V7X_LANES))
    lg_spec = pl.BlockSpec((None, 1, V7X_LANES), lambda b, h: (h, 0, 0))

    def specs(t):
        return [pl.BlockSpec((t, dk), lambda b, h: (b, h)),
                pl.BlockSpec((t, dk), lambda b, h: (b, heads + h)),
                pl.BlockSpec((t, dv), lambda b, h: (b, heads + h)),
                pl.BlockSpec((t, dv), lambda b, h: (b, 2 * heads + h))]

    out_spec = lambda t: pl.BlockSpec((t, dv), lambda b, h: (b, h))
    grid = (batch, heads)
    cast_in, cast_out, cast_shapes = _cast_plan(casts, grid)
    out = pl.pallas_call(
        functools.partial(_retention_kernel, chunk=c, reorders=tuple(w[4] for w in casts)),
        grid=grid,
        in_specs=[lg_spec, lg_spec] + specs(t_c) + specs(t_l) + cast_in,
        out_specs=[out_spec(t_c), out_spec(t_l)] + cast_out,
        out_shape=[jax.ShapeDtypeStruct((p_ctx.shape[0], heads * dv), BF16),
                   jax.ShapeDtypeStruct((p_lat.shape[0], heads * dv), BF16)] + cast_shapes,
        scratch_shapes=[pltpu.VMEM((dk, dv), F32), pltpu.VMEM((dk, dv), F32), pltpu.VMEM((c, c), F32),
                        pltpu.VMEM((4, c, dk), BF16),
                        pltpu.VMEM((t_c, dv), F32), pltpu.VMEM((t_l, dv), F32)],
        compiler_params=_params("arbitrary", "arbitrary"),
        name="retention",
    )(lanes(lg_fwd), lanes(lg_bwd), p_ctx, p_ctx, p_ctx, p_ctx, p_lat, p_lat, p_lat, p_lat,
      *[w[0] for w in casts])
    return out[0], out[1], out[2:]


def _outproj_kernel(*refs, n_parts, seg):
    a_refs, w_refs = refs[:n_parts], refs[n_parts:2 * n_parts]
    res_ref, gate_ref, o_ref = refs[2 * n_parts:]
    for c0 in range(0, o_ref.shape[1], seg):
        cols = slice(c0, c0 + seg)
        acc = functools.reduce(jnp.add, [jnp.dot(a[...], w[:, cols], preferred_element_type=F32)
                                         for a, w in zip(a_refs, w_refs)])
        o_ref[:, cols] = res_ref[:, cols] + gate_ref[:, cols] * acc


def _outproj(parts, weights, res, gate, rows_per_mod):
    n, d = res.shape
    tm = _tile(rows_per_mod, 512)
    return pl.pallas_call(
        functools.partial(_outproj_kernel, n_parts=len(parts), seg=_tile(d, 512, V7X_LANES)),
        grid=(n // tm,),
        in_specs=[_row_spec(tm, a.shape[1]) for a in parts]
        + [pl.BlockSpec((None, w.shape[1], d), lambda i: (0, 0, 0), pipeline_mode=pl.Buffered(1))
           for w in weights]
        + [_row_spec(tm, d), _mod_spec(gate, rows_per_mod, tm)],
        out_specs=_row_spec(tm, d),
        out_shape=jax.ShapeDtypeStruct((n, d), F32),
        compiler_params=_params("parallel"),
        name="outproj",
    )(*parts, *weights, res, gate)


def _ffn_kernel(x_ref, sh_ref, sc_ref, gate_ref, wg_ref, wu_ref, wo_ref, o_ref, u_scr, *, n_steps):
    j = pl.program_id(1)

    @pl.when(j == 0)
    def _():
        _modulate_rows(x_ref, sh_ref, sc_ref, u_scr)
        o_ref[...] = jnp.zeros_like(o_ref)

    u = u_scr[...]
    g = jnp.dot(u, wg_ref[...], preferred_element_type=F32)
    up = jnp.dot(u, wu_ref[...], preferred_element_type=F32)
    act = (_silu(g) * up).astype(BF16)
    o_ref[...] += jnp.dot(act, wo_ref[...], preferred_element_type=F32)

    @pl.when(j == n_steps - 1)
    def _():
        o_ref[...] = x_ref[...] + gate_ref[...] * o_ref[...]


def _ffn(h, shift, scale, gate, rows_per_mod, w_in, w_out):
    n, d = h.shape
    f = w_out.shape[1]
    tm = _tile(rows_per_mod, 1024)
    tf = _tile(f, 512, V7X_LANES)
    nf = f // tf
    n_tiles = n // tm
    return pl.pallas_call(
        functools.partial(_ffn_kernel, n_steps=nf),
        grid=(n_tiles, nf),
        in_specs=[
            _row_spec(tm, d),
            _mod_spec(shift, rows_per_mod, tm), _mod_spec(scale, rows_per_mod, tm),
            _mod_spec(gate, rows_per_mod, tm),
            pl.BlockSpec((None, d, tf), lambda i, j: (0, 0, j)),
            pl.BlockSpec((None, d, tf), lambda i, j: (0, 0, nf + j)),
            pl.BlockSpec((None, tf, d), lambda i, j: (0, j, 0)),
        ],
        out_specs=_row_spec(tm, d),
        out_shape=jax.ShapeDtypeStruct((n, d), F32),
        scratch_shapes=[pltpu.VMEM((tm, d), BF16)],
        compiler_params=_params("parallel", "arbitrary"),
        name="ffn",
    )(h, shift, scale, gate, w_in, w_in, w_out)


def _even_layer(h_ctx, h_lat, mod_ctx, mod_lat, weights, q_gain, k_gain, batch, need_ctx, tables, casts):
    n_ctx = h_ctx.shape[0]
    n_lat = h_lat.shape[0]
    hd = ATTN_HEAD_DIM
    w_in = weights[0]
    q_gain, k_gain = _reorder_gain(q_gain).reshape(1, hd), _reorder_gain(k_gain).reshape(1, hd)
    kvdim = ATTN_KV_HEADS * hd
    fdim = FOURIER_GROUPS * (h_lat.shape[1] // 16)
    dims = (fdim, w_in.shape[2] - fdim - 2 * kvdim, kvdim)

    f_c, q_c, k_c, v_c = _proj_even(h_ctx, mod_ctx[0], mod_ctx[1], n_ctx, w_in, q_gain, k_gain,
                                    *tables["rope_even_ctx"], dims)
    f_l, q_l, k_l, v_l = _proj_even(h_lat, mod_lat[0], mod_lat[1], n_lat // batch, w_in, q_gain, k_gain,
                                    *tables["rope_even_lat"], dims)
    a_l, cast_out = _attention(q_l, [(k_l, v_l), (k_c, v_c)], batch, casts)
    weights = list(weights) + list(cast_out)
    w_f, w_a, ffn_weights, next_weights = weights[1], weights[2], weights[3:5], weights[5:]
    m_l = _fourier(f_l, batch, tables["dft_lat"])
    h_lat = _outproj([m_l, a_l], [w_f, w_a], h_lat, mod_lat[2], n_lat // batch)
    if need_ctx:
        a_c, _ = _attention(q_c, [(k_c, v_c)], batch)
        m_c = _fourier(f_c, batch, tables["dft_ctx"])
        h_ctx = _outproj([m_c, a_c], [w_f, w_a], h_ctx, mod_ctx[2], n_ctx)
    return h_ctx, h_lat, ffn_weights, next_weights


def _odd_layer(h_ctx, h_lat, mod_ctx, mod_lat, weights, lg_fwd, lg_bwd, batch, need_ctx, tables, casts):
    n_ctx = h_ctx.shape[0]
    n_lat = h_lat.shape[0]
    w_in = weights[0]
    qk_dim = w_in.shape[2] // 6
    p_c = _proj_odd(h_ctx, mod_ctx[0], mod_ctx[1], n_ctx, w_in, tables["rope_odd_ctx"], qk_dim)
    p_l = _proj_odd(h_lat, mod_lat[0], mod_lat[1], n_lat // batch, w_in, tables["rope_odd_lat"], qk_dim)
    m_c, m_l, cast_out = _retention(p_c, p_l, lg_fwd, lg_bwd, batch, casts)
    weights = list(weights) + list(cast_out)
    w_out, ffn_weights, next_weights = weights[1], weights[2:4], weights[4:]
    h_lat = _outproj([m_l], [w_out], h_lat, mod_lat[2], n_lat // batch)
    if need_ctx:
        h_ctx = _outproj([m_c], [w_out], h_ctx, mod_ctx[2], n_ctx)
    return h_ctx, h_lat, ffn_weights, next_weights


def kernel(x, c, ctx, c_ctx, w_mod, b_mod, w_in_even, w_out_even, q_gain_even, k_gain_even,
           w_in_odd, w_out_odd, log_decay_fwd, log_decay_bwd, w_ffn_in, w_ffn_out):
    batch, seq, d = x.shape
    ctx_len = ctx.shape[1]
    depth = w_mod.shape[0]
    n_lat, n_ctx = batch * seq, batch * ctx_len
    h_lat, h_ctx = x.reshape(n_lat, d), ctx.reshape(n_ctx, d)

    rows = -(-(batch + 1) // 8) * 8
    c_all = jnp.concatenate([c, c_ctx[None], jnp.zeros((rows - batch - 1, d), F32)], axis=0)
    mods = _modulation(c_all, w_mod, b_mod).reshape(depth, rows, N_MOD, d)

    fdim = FOURIER_GROUPS * (d // 16)

    def weight_slices(i):
        j = i // 2
        if i % 2 == 0:
            qk_hi = w_in_even.shape[2] - ATTN_KV_HEADS * ATTN_HEAD_DIM
            mixer = [(w_in_even, j, 0, d, (ATTN_HEAD_DIM, fdim, qk_hi)),
                     (w_out_even, j, 0, fdim, None), (w_out_even, j, fdim, w_out_even.shape[1], None)]
        else:
            mixer = [(w_in_odd, j, 0, d, (ret_hd, 0, w_in_odd.shape[2] // 3)),
                     (w_out_odd, j, 0, w_out_odd.shape[1], None)]
        return mixer + [(w_ffn_in, i, 0, d, None), (w_ffn_out, i, 0, w_ffn_out.shape[1], None)]

    ret_hd = w_in_odd.shape[2] // (6 * log_decay_fwd.shape[1])
    ctx_rows = _tile(n_ctx, 1024)
    tables = {
        "rope_even_lat": _rope_tables(seq, ATTN_HEAD_DIM, False),
        "rope_even_ctx": _rope_tables(ctx_rows, ATTN_HEAD_DIM, True),
        "rope_odd_lat": _proj_odd_tables(*_rope_tables(seq, ret_hd, False), ret_hd ** -0.5),
        "rope_odd_ctx": _proj_odd_tables(*_rope_tables(ctx_rows, ret_hd, True), ret_hd ** -0.5),
        "dft_lat": _dft_tables(seq, d // 16),
        "dft_ctx": _dft_tables(ctx_len, d // 16),
    }

    first = weight_slices(0)
    weights, late = [_to_bf16(*first[0])], first[1:]
    for i in range(depth):
        need_ctx = i < depth - 1
        casts = late + (weight_slices(i + 1) if i + 1 < depth else [])
        mod_lat = [mods[i, :batch, k][:, None, :] for k in range(N_MOD)]
        mod_ctx = [mods[i, batch:batch + 1, k][:, None, :] for k in range(N_MOD)]
        j = i // 2
        if i % 2 == 0:
            h_ctx, h_lat, ffn_weights, weights = _even_layer(
                h_ctx, h_lat, mod_ctx, mod_lat, weights, q_gain_even[j], k_gain_even[j], batch, need_ctx,
                tables, casts)
        else:
            h_ctx, h_lat, ffn_weights, weights = _odd_layer(
                h_ctx, h_lat, mod_ctx, mod_lat, weights, log_decay_fwd[j], log_decay_bwd[j], batch, need_ctx,
                tables, casts)
        late = []
        h_lat = _ffn(h_lat, mod_lat[3], mod_lat[4], mod_lat[5], n_lat // batch, *ffn_weights)
        if need_ctx:
            h_ctx = _ffn(h_ctx, mod_ctx[3], mod_ctx[4], mod_ctx[5], n_ctx, *ffn_weights)
    return h_lat.reshape(batch, seq, d)
```

```python
import functools
import math

import jax
import jax.numpy as jnp
import numpy as np
from jax import lax
from jax.experimental import pallas as pl
from jax.experimental.pallas import tpu as pltpu

F32 = jnp.float32
BF16 = jnp.bfloat16

EPS = 1e-6
ROPE_BASE = 10000.0
GRID_W = 64
N_MOD = 6
FOURIER_GROUPS = 4
ATTN_HEAD_DIM = 128
ATTN_KV_HEADS = 4

V7X_LANES = 128
V7X_VMEM_BYTES = 64 * 1024 * 1024
VMEM_LIMIT_BYTES = V7X_VMEM_BYTES - 8 * 1024 * 1024


def _tile(n, target, align=8):
    best = None
    for t in range(align, min(n, target) + 1, align):
        if n % t == 0:
            best = t
    return n if best is None else best


def _params(*semantics):
    return pltpu.CompilerParams(dimension_semantics=semantics, vmem_limit_bytes=VMEM_LIMIT_BYTES)


def _silu(x):
    return x * jax.nn.sigmoid(x)


def _norm_mod(x, shift, scale):
    ms = jnp.mean(x * x, axis=-1, keepdims=True)
    return x * lax.rsqrt(ms + EPS) * (1.0 + scale) + shift


def _next_tile(i, n_tiles):
    return jnp.minimum(i + 1, n_tiles - 1)


def _row_spec(tm, width, n_tiles=None):
    if n_tiles is None:
        return pl.BlockSpec((tm, width), lambda i, *_: (i, 0))
    return pl.BlockSpec((tm, width), lambda i, *_: (_next_tile(i, n_tiles), 0))


def _mod_spec(mod, rows_per_mod, tm, n_tiles=None):
    tiles = rows_per_mod // tm
    d = mod.shape[-1]
    if n_tiles is None:
        return pl.BlockSpec((None, 1, d), lambda i, *_: (i // tiles, 0, 0))
    return pl.BlockSpec((None, 1, d), lambda i, *_: (_next_tile(i, n_tiles) // tiles, 0, 0))


ROW_CHUNK = 16
LANE_CHUNK = 1024


def _modulate_rows(x_ref, sh_ref, sc_ref, dst, start=0, n_rows=None):
    n_rows = x_ref.shape[0] if n_rows is None else n_rows
    d = x_ref.shape[1]
    for r0 in range(0, n_rows, ROW_CHUNK):
        first = start + r0
        rows = pl.ds(first if isinstance(first, int) else pl.multiple_of(first, ROW_CHUNK), ROW_CHUNK)
        x = x_ref[rows, :]
        inv = lax.rsqrt(jnp.mean(x * x, axis=-1, keepdims=True) + EPS)
        for c0 in range(0, d, LANE_CHUNK):
            cols = slice(c0, min(c0 + LANE_CHUNK, d))
            y = x_ref[rows, cols] * inv * (1.0 + sc_ref[:, cols]) + sh_ref[:, cols]
            dst[rows, cols] = y.astype(BF16)


def _stage_modulated(x_ref, sh_ref, sc_ref, u_scr, slot, step, n_steps):
    tm = x_ref.shape[0]
    if n_steps == 1:
        _modulate_rows(x_ref, sh_ref, sc_ref, u_scr.at[slot])
    else:
        r = min(tm, -(-(-(-tm // n_steps)) // ROW_CHUNK) * ROW_CHUNK)
        start = pl.multiple_of(jnp.minimum(step * r, tm - r), ROW_CHUNK)
        _modulate_rows(x_ref, sh_ref, sc_ref, u_scr.at[slot], start, r)


def _mod_kernel(c_ref, w_ref, b_ref, o_ref):
    cond = _silu(c_ref[...]).astype(BF16)
    o_ref[...] = jnp.dot(cond, w_ref[...].astype(BF16), preferred_element_type=F32) + b_ref[...]


def _modulation(c_all, w_mod, b_mod):
    depth, d, n = w_mod.shape
    r = c_all.shape[0]
    tn = _tile(n, 1024, V7X_LANES)
    return pl.pallas_call(
        _mod_kernel,
        grid=(depth, n // tn),
        in_specs=[
            pl.BlockSpec((r, d), lambda i, j: (0, 0)),
            pl.BlockSpec((None, d, tn), lambda i, j: (i, 0, j)),
            pl.BlockSpec((None, 1, tn), lambda i, j: (i, 0, j)),
        ],
        out_specs=pl.BlockSpec((None, r, tn), lambda i, j: (i, 0, j)),
        out_shape=jax.ShapeDtypeStruct((depth, r, n), F32),
        compiler_params=_params("parallel", "parallel"),
        name="modulation",
    )(c_all, w_mod, b_mod.reshape(depth, 1, n))


def _reorder_head_dims(x, hd):
    half = V7X_LANES // 2
    if hd == 2 * V7X_LANES:
        t1, t2 = x[:, :V7X_LANES], x[:, V7X_LANES:]
        low = lax.broadcasted_iota(jnp.int32, t1.shape, 1) < half
        return jnp.concatenate([jnp.where(low, t1, pltpu.roll(t2, half, 1)),
                                jnp.where(low, pltpu.roll(t1, half, 1), t2)], axis=1)
    assert hd == V7X_LANES, hd
    lane = lax.broadcasted_iota(jnp.int32, x.shape, 1)
    quarter = hd // 4
    return jnp.where((lane >= quarter) & (lane < 2 * quarter), pltpu.roll(x, hd - quarter, 1),
                     jnp.where((lane >= 2 * quarter) & (lane < 3 * quarter), pltpu.roll(x, quarter, 1), x))


def _cast_block(x_ref, o_ref, reorder):
    if reorder is None:
        o_ref[...] = x_ref[...].astype(BF16)
        return
    hd, lo, hi = reorder
    n = x_ref.shape[1]
    if lo > 0:
        o_ref[:, :lo] = x_ref[:, :lo].astype(BF16)
    for h0 in range(lo, hi, hd):
        o_ref[:, h0:h0 + hd] = _reorder_head_dims(x_ref[:, h0:h0 + hd], hd).astype(BF16)
    if hi < n:
        o_ref[:, hi:] = x_ref[:, hi:].astype(BF16)


def _cast_kernel(x_ref, o_ref, *, reorder):
    _cast_block(x_ref, o_ref, reorder)


def _to_bf16(w, layer, row_lo, row_hi, reorder=None):
    n = w.shape[2]
    rows = row_hi - row_lo
    tn = n if reorder is not None else _tile(n, 4096, V7X_LANES)
    tk = _tile(math.gcd(rows, row_lo), max(16, 256 * 4096 // tn), 16)
    first = row_lo // tk
    return pl.pallas_call(
        functools.partial(_cast_kernel, reorder=reorder),
        grid=(rows // tk, n // tn),
        in_specs=[pl.BlockSpec((None, tk, tn), lambda r, c: (layer, first + r, c))],
        out_specs=pl.BlockSpec((None, tk, tn), lambda r, c: (0, r, c)),
        out_shape=jax.ShapeDtypeStruct((1, rows, n), BF16),
        compiler_params=_params("parallel", "parallel"),
        name="to_bf16",
    )(w)


def _cast_plan(casts, grid):
    steps = math.prod(grid)

    def step_of(*idx):
        s = 0
        for i, g in zip(idx, grid):
            s = s * g + i
        return s

    in_specs, out_specs, out_shapes = [], [], []
    for w, layer, lo, hi, _ in casts:
        rows, n = hi - lo, w.shape[2]
        unit = math.gcd(rows, lo)
        rb = min(t for t in range(16, unit + 1, 16) if unit % t == 0 and rows // t <= steps)
        nb, first = rows // rb, lo // rb
        in_specs.append(pl.BlockSpec(
            (None, rb, n),
            lambda *idx, layer=layer, first=first, nb=nb: (layer, first + jnp.minimum(step_of(*idx), nb - 1), 0)))
        out_specs.append(pl.BlockSpec(
            (None, rb, n), lambda *idx, nb=nb: (0, jnp.minimum(step_of(*idx), nb - 1), 0)))
        out_shapes.append(jax.ShapeDtypeStruct((1, rows, n), BF16))
    return in_specs, out_specs, out_shapes


def _run_casts(reorders, in_refs, out_refs):
    for reorder, x_ref, o_ref in zip(reorders, in_refs, out_refs):
        _cast_block(x_ref, o_ref, reorder)


def _rope_angles(n_tokens, dp):
    t = np.arange(n_tokens)
    inv = ROPE_BASE ** (-np.arange(0, dp, 2, dtype=np.float64) / dp)
    return (t // GRID_W)[:, None] * inv[None, :], (t % GRID_W)[:, None] * inv[None, :]


def _rope_tables(n_tokens, head_dim, identity):
    if identity:
        return jnp.ones((n_tokens, head_dim), F32), jnp.zeros((n_tokens, head_dim), F32)
    row, col = _rope_angles(n_tokens, head_dim // 2)
    cos = np.concatenate([np.cos(row), np.cos(col), np.cos(row), np.cos(col)], axis=-1)
    sin = np.concatenate([-np.sin(row), -np.sin(col), np.sin(row), np.sin(col)], axis=-1)
    return jnp.asarray(cos, F32), jnp.asarray(sin, F32)


def _reorder_gain(gain):
    g = gain.reshape(4, -1)
    return jnp.concatenate([g[0], g[2], g[1], g[3]])


def _proj_even_kernel(x_ref, sh_ref, sc_ref, xn_ref, shn_ref, scn_ref, w_ref, qg_ref, kg_ref, cos_ref, sin_ref,
                      f_ref, q_ref, k_ref, v_ref, u_scr, *, fdim, qdim, kvdim, hd):
    i = pl.program_id(0)
    slot = i % 2

    @pl.when(i == 0)
    def _():
        _stage_modulated(x_ref, sh_ref, sc_ref, u_scr, 0, 0, 1)

    u = u_scr[slot]
    cos = cos_ref[...]
    sin = sin_ref[...]

    def heads(lo, width, gain_ref, out_ref):
        seg = _tile(width, 512, hd)
        gain = gain_ref[...]
        for s in range(width // seg):
            y = jnp.dot(u, w_ref[:, lo + s * seg:lo + (s + 1) * seg], preferred_element_type=F32)
            for h in range(seg // hd):
                yh = y[:, h * hd:(h + 1) * hd]
                ms = jnp.mean(yh * yh, axis=-1, keepdims=True)
                yh = yh * lax.rsqrt(ms + EPS) * gain
                yh = yh * cos + pltpu.roll(yh, hd // 2, 1) * sin
                out_ref[:, s * seg + h * hd:s * seg + (h + 1) * hd] = yh.astype(BF16)

    heads(fdim, qdim, qg_ref, q_ref)
    heads(fdim + qdim, kvdim, kg_ref, k_ref)
    f_ref[...] = jnp.dot(u, w_ref[:, :fdim], preferred_element_type=F32).astype(BF16)
    v_ref[...] = jnp.dot(u, w_ref[:, fdim + qdim + kvdim:], preferred_element_type=F32).astype(BF16)
    _stage_modulated(xn_ref, shn_ref, scn_ref, u_scr, 1 - slot, 0, 1)


def _first_tile_specs(tm, d, shift, scale):
    once = pl.Buffered(1)
    return [pl.BlockSpec((tm, d), lambda i, *_: (0, 0), pipeline_mode=once),
            pl.BlockSpec((None, 1, d), lambda i, *_: (0, 0, 0), pipeline_mode=once),
            pl.BlockSpec((None, 1, d), lambda i, *_: (0, 0, 0), pipeline_mode=once)]


def _proj_even(h, shift, scale, rows_per_mod, w, q_gain, k_gain, cos, sin, dims):
    n, d = h.shape
    fdim, qdim, kvdim = dims
    hd = ATTN_HEAD_DIM
    pos_rows = cos.shape[0]
    tm = _tile(math.gcd(rows_per_mod, pos_rows), 512)
    n_tiles = n // tm
    pos_tiles = pos_rows // tm
    kern = functools.partial(_proj_even_kernel, fdim=fdim, qdim=qdim, kvdim=kvdim, hd=hd)
    const = lambda a: pl.BlockSpec(a.shape, lambda i: (0,) * a.ndim)
    return pl.pallas_call(
        kern,
        grid=(n_tiles,),
        in_specs=_first_tile_specs(tm, d, shift, scale) + [
            _row_spec(tm, d, n_tiles),
            _mod_spec(shift, rows_per_mod, tm, n_tiles), _mod_spec(scale, rows_per_mod, tm, n_tiles),
            pl.BlockSpec((None,) + w.shape[1:], lambda i: (0, 0, 0)),
            const(q_gain), const(k_gain),
            pl.BlockSpec((tm, hd), lambda i: (i % pos_tiles, 0)),
            pl.BlockSpec((tm, hd), lambda i: (i % pos_tiles, 0)),
        ],
        out_specs=[_row_spec(tm, fdim), _row_spec(tm, qdim), _row_spec(tm, kvdim), _row_spec(tm, kvdim)],
        out_shape=[jax.ShapeDtypeStruct((n, fdim), BF16), jax.ShapeDtypeStruct((n, qdim), BF16),
                   jax.ShapeDtypeStruct((n, kvdim), BF16), jax.ShapeDtypeStruct((n, kvdim), BF16)],
        scratch_shapes=[pltpu.VMEM((2, tm, d), BF16)],
        compiler_params=_params("arbitrary"),
        name="proj_even",
    )(h, shift, scale, h, shift, scale, w, q_gain, k_gain, cos, sin)


def _attn_kernel(q_ref, *refs, n_seg, reorders, group, hd, scale, n_split):
    n_cast = len(reorders)
    k_refs, v_refs = refs[:n_seg], refs[n_seg:2 * n_seg]
    cast_in, refs = refs[2 * n_seg:2 * n_seg + n_cast], refs[2 * n_seg + n_cast:]
    o_ref, cast_out = refs[0], refs[1:1 + n_cast]
    s_scr, p_scr, v_scr = refs[1 + n_cast:]
    tq = q_ref.shape[0]
    _run_casts(reorders, cast_in, cast_out)

    @pl.when(pl.program_id(2) == 0)
    def _():
        off = 0
        for v in v_refs:
            n = v.shape[0]
            v_scr[off:off + n, :hd] = v[...]
            v_scr[off:off + n, hd:] = jnp.ones((n, hd), BF16)
            off += n

    q = q_ref[...]
    qs = jnp.concatenate([q[:, g * hd:(g + 1) * hd] for g in range(group)], axis=0)
    nt = (((1,), (1,)), ((), ()))
    rows = qs.shape[0] // n_split
    blocks = [slice(i * rows, (i + 1) * rows) for i in range(n_split)]
    for blk in blocks:
        off = 0
        for k in k_refs:
            n = k.shape[0]
            s_scr[blk, off:off + n] = lax.dot_general(qs[blk], k[...], nt, preferred_element_type=F32)
            off += n
    outs = []
    for blk in blocks:
        m = jnp.max(s_scr[blk, :], axis=-1, keepdims=True)
        p_scr[blk, :] = jnp.exp2((s_scr[blk, :] - m) * (scale * math.log2(math.e))).astype(BF16)
    for blk in blocks:
        o = jnp.dot(p_scr[blk, :], v_scr[...], preferred_element_type=F32)
        outs.append(o[:, :hd] / o[:, hd:])
    o = jnp.concatenate(outs, axis=0)
    for g in range(group):
        o_ref[:, g * hd:(g + 1) * hd] = o[g * tq:(g + 1) * tq].astype(BF16)


def _attention(q, segments, batch, casts=()):
    hd, kvh = ATTN_HEAD_DIM, ATTN_KV_HEADS
    n, qdim = q.shape
    group = qdim // (kvh * hd)
    t_q = n // batch
    tq = _tile(t_q, 512)
    nq = t_q // tq
    n_keys = sum(k.shape[0] for k, _ in segments) // batch
    grid = (batch, kvh, nq)
    cast_in, cast_out, cast_shapes = _cast_plan(casts, grid)
    kern = functools.partial(_attn_kernel, n_seg=len(segments), reorders=tuple(w[4] for w in casts), group=group, hd=hd,
                             scale=hd ** -0.5, n_split=4)
    kv_spec = lambda a: pl.BlockSpec((a.shape[0] // batch, hd), lambda b, kh, t: (b, kh))
    q_spec = pl.BlockSpec((tq, group * hd), lambda b, kh, t: (b * nq + t, kh))
    out = pl.pallas_call(
        kern,
        grid=grid,
        in_specs=[q_spec] + [kv_spec(k) for k, _ in segments] + [kv_spec(v) for _, v in segments] + cast_in,
        out_specs=[q_spec] + cast_out,
        out_shape=[jax.ShapeDtypeStruct((n, qdim), BF16)] + cast_shapes,
        scratch_shapes=[pltpu.VMEM((group * tq, n_keys), F32), pltpu.VMEM((group * tq, n_keys), BF16),
                        pltpu.VMEM((n_keys, 2 * hd), BF16)],
        compiler_params=_params("arbitrary", "arbitrary", "arbitrary"),
        name="attention",
    )(q, *[k for k, _ in segments], *[v for _, v in segments], *[w[0] for w in casts])
    return out[0], out[1:]


def _fourier_kernel(u_ref, cm_ref, cse_ref, cso_ref, o_ref, ze_scr, zo_scr, *, groups, gd, scale):
    h, fdim = u_ref.shape[0], u_ref.shape[1] // 2
    t = 2 * h
    tr = cse_ref.shape[0]
    r = pl.program_id(1)

    @pl.when(r == 0)
    def _():
        for parity, z_scr in ((0, ze_scr), (1, zo_scr)):
            for g in range(groups):
                cols = slice(g * gd, (g + 1) * gd)
                ug = u_ref[:, parity * fdim + g * gd:parity * fdim + (g + 1) * gd]
                z = jnp.dot(ug, cm_ref[...], preferred_element_type=F32)
                z_scr[0:h, cols] = z[:, :gd].astype(BF16)
                z_scr[h:t, cols] = z[:, gd:].astype(BF16)

    e = jnp.dot(cse_ref[...], ze_scr[...], preferred_element_type=F32)
    o = jnp.dot(cso_ref[...], zo_scr[...], preferred_element_type=F32)
    o_ref[pl.ds(pl.multiple_of(r * tr, tr), tr), :] = ((e + o) * scale).astype(BF16)
    o_ref[pl.ds(pl.multiple_of(h + r * tr, tr), tr), :] = ((e - o) * scale).astype(BF16)


def _dft_tables(t, gd):
    def table(rows, cols, n, transform):
        a = ((rows[:, None] * cols[None, :]) % n) * (2.0 * math.pi / n)
        return np.concatenate([np.cos(a), transform(np.sin(a))], axis=1).astype(BF16)
    c, r = np.arange(gd), np.arange(t // 2)
    cm = table(c, c, gd, np.negative)
    cse = table(r, 2 * r, t, np.asarray)
    cso = table(r, 2 * r + 1, t, np.asarray)
    return jnp.asarray(cm), jnp.asarray(cse), jnp.asarray(cso)


def _fourier(u, batch, tables):
    n, fdim = u.shape
    t = n // batch
    gd = fdim // FOURIER_GROUPS
    cm, cse, cso = tables
    tr = _tile(t // 2, 512)
    kern = functools.partial(_fourier_kernel, groups=FOURIER_GROUPS, gd=gd, scale=(t * gd) ** -0.5)
    half_rows = pl.BlockSpec((tr, t), lambda b, r: (r, 0))
    return pl.pallas_call(
        kern,
        grid=(batch, t // 2 // tr),
        in_specs=[
            pl.BlockSpec((t // 2, 2 * fdim), lambda b, r: (b, 0)),
            pl.BlockSpec(cm.shape, lambda b, r: (0, 0)),
            half_rows, half_rows,
        ],
        out_specs=pl.BlockSpec((t, fdim), lambda b, r: (b, 0)),
        out_shape=jax.ShapeDtypeStruct((n, fdim), BF16),
        scratch_shapes=[pltpu.VMEM((t, fdim), BF16), pltpu.VMEM((t, fdim), BF16)],
        compiler_params=_params("parallel", "arbitrary"),
        name="fourier",
    )(u.reshape(n // 2, 2 * fdim), cm, cse, cso)


def _proj_odd_kernel(x_ref, sh_ref, sc_ref, xn_ref, shn_ref, scn_ref, w_ref, ta_ref, tb_ref, o_ref, u_scr,
                     *, n_steps, hd):
    i, j = pl.program_id(0), pl.program_id(1)
    slot = i % 2

    @pl.when((i == 0) & (j == 0))
    def _():
        _stage_modulated(x_ref, sh_ref, sc_ref, u_scr, 0, 0, 1)

    u = u_scr[slot]
    half = hd // 2
    for c0 in range(0, o_ref.shape[1], hd):
        y = jnp.dot(u, w_ref[:, c0:c0 + hd], preferred_element_type=F32)
        t1, t2 = y[:, :half], y[:, half:]
        o_ref[:, c0:c0 + half] = (t1 * ta_ref[:, :half] + t2 * tb_ref[:, :half]).astype(BF16)
        o_ref[:, c0 + half:c0 + hd] = (t2 * ta_ref[:, half:] + t1 * tb_ref[:, half:]).astype(BF16)
    _stage_modulated(xn_ref, shn_ref, scn_ref, u_scr, 1 - slot, j, n_steps)


def _proj_odd_tables(cos, sin, k_scale):
    a = jnp.stack([cos, cos * k_scale, jnp.ones_like(cos)])
    b = jnp.stack([sin, sin * k_scale, jnp.zeros_like(sin)])
    return a, b


def _proj_odd(h, shift, scale, rows_per_mod, w, tables, qk_dim, columns=None):
    n, d = h.shape
    n_out = w.shape[2]
    ta, tb = tables
    pos_rows, hd = ta.shape[1:]
    tm = _tile(math.gcd(rows_per_mod, pos_rows), 1024)
    tn = _tile(math.gcd(qk_dim, n_out), 1024, hd)
    n_tiles = n // tm
    pos_tiles = pos_rows // tm
    q_tiles = qk_dim // tn
    lo, hi = (0, n_out) if columns is None else columns
    j0, n_steps = lo // tn, (hi - lo) // tn
    table_spec = pl.BlockSpec((None, tm, hd),
                              lambda i, j: (jnp.minimum((j0 + j) // q_tiles, 2), i % pos_tiles, 0))
    return pl.pallas_call(
        functools.partial(_proj_odd_kernel, n_steps=n_steps, hd=hd),
        grid=(n_tiles, n_steps),
        in_specs=_first_tile_specs(tm, d, shift, scale) + [
            _row_spec(tm, d, n_tiles),
            _mod_spec(shift, rows_per_mod, tm, n_tiles), _mod_spec(scale, rows_per_mod, tm, n_tiles),
            pl.BlockSpec((None, d, tn), lambda i, j: (0, 0, j0 + j)),
            table_spec, table_spec,
        ],
        out_specs=pl.BlockSpec((tm, tn), lambda i, j: (i, j)),
        out_shape=jax.ShapeDtypeStruct((n, hi - lo), BF16),
        scratch_shapes=[pltpu.VMEM((2, tm, d), BF16)],
        compiler_params=_params("arbitrary", "arbitrary"),
        name="proj_odd",
    )(h, shift, scale, h, shift, scale, w, ta, tb)


def _retention_kernel(lgf_ref, lgb_ref, *refs, chunk, reorders, ctx_out):
    n_cast = len(reorders)
    n_ctx = 4 if ctx_out else 2
    ctx_refs, (ql_ref, kl_ref, vl_ref, gl_ref) = refs[:n_ctx], refs[n_ctx:n_ctx + 4]
    cast_in, refs = refs[n_ctx + 4:n_ctx + 4 + n_cast], refs[n_ctx + 4 + n_cast:]
    out_refs, refs = refs[:1 + ctx_out], refs[1 + ctx_out:]
    cast_out, refs = refs[:n_cast], refs[n_cast:]
    sf_scr, sb_scr, d_scr, dec_scr = refs[:4]
    part_scrs = refs[4:]
    _run_casts(reorders, cast_in, cast_out)
    c = chunk
    dk = sf_scr.shape[0]
    lgf = lgf_ref[...][:, :1]
    lgb = lgb_ref[...][:, :1]
    diff = (lax.broadcasted_iota(jnp.int32, (c, c), 0)
            - lax.broadcasted_iota(jnp.int32, (c, c), 1)).astype(F32)
    d_scr[...] = jnp.where(diff >= 0.0, jnp.exp(jnp.maximum(diff, 0.0) * lgf),
                           jnp.exp(jnp.maximum(-diff, 0.0) * lgb))
    idx = lax.broadcasted_iota(jnp.int32, (c, dk), 0).astype(F32)
    dec_scr[0] = jnp.exp((idx + 1.0) * lgf).astype(BF16)
    dec_scr[1] = jnp.exp((c - 1.0 - idx) * lgf).astype(BF16)
    dec_scr[2] = jnp.exp((c - idx) * lgb).astype(BF16)
    dec_scr[3] = jnp.exp(idx * lgb).astype(BF16)
    chunk_dec_f, chunk_dec_b = jnp.exp(c * lgf), jnp.exp(c * lgb)
    nt = (((1,), (1,)), ((), ()))
    tn = (((0,), (0,)), ((), ()))

    sf_scr[...] = jnp.zeros_like(sf_scr)
    sb_scr[...] = jnp.zeros_like(sb_scr)

    def advance(state_scr, k, v, k_dec, chunk_dec):
        state_scr[...] = state_scr[...] * chunk_dec + lax.dot_general(k * k_dec, v, tn,
                                                                      preferred_element_type=F32)

    def forward(seg, i):
        q_ref, k_ref, v_ref = seg[:3]
        rows = slice(i * c, (i + 1) * c)
        q, k, v = q_ref[rows, :], k_ref[rows, :], v_ref[rows, :]
        s = lax.dot_general(q, k, nt, preferred_element_type=F32) * d_scr[...]
        lhs = jnp.concatenate([s.astype(BF16), q * dec_scr[0]], axis=1)
        rhs = jnp.concatenate([v, sf_scr[...].astype(BF16)], axis=0)
        o = jnp.dot(lhs, rhs, preferred_element_type=F32)
        advance(sf_scr, k, v, dec_scr[1], chunk_dec_f)
        return o

    def backward(seg, i):
        q_ref, k_ref, v_ref = seg[:3]
        rows = slice(i * c, (i + 1) * c)
        q, k, v = q_ref[rows, :], k_ref[rows, :], v_ref[rows, :]
        o = jnp.dot(q * dec_scr[2], sb_scr[...].astype(BF16), preferred_element_type=F32)
        advance(sb_scr, k, v, dec_scr[3], chunk_dec_b)
        return o

    def finish(seg, i, o):
        g_ref, o_ref = seg[3], seg[4]
        rows = slice(i * c, (i + 1) * c)
        o = o * lax.rsqrt(jnp.mean(o * o, axis=-1, keepdims=True) + EPS)
        o_ref[rows, :] = _silu(g_ref[rows, :]) * o.astype(BF16)

    def sweep(seg):
        part = seg[5]
        n = seg[0].shape[0] // c
        for t in range(n):
            i_f, i_b = t, n - 1 - t
            o_f, o_b = forward(seg, i_f), backward(seg, i_b)
            if i_f == i_b:
                finish(seg, i_f, o_f + o_b)
            elif i_f < i_b:
                part[i_f * c:(i_f + 1) * c, :] = o_f
                part[i_b * c:(i_b + 1) * c, :] = o_b
            else:
                finish(seg, i_f, o_f + part[i_f * c:(i_f + 1) * c, :])
                finish(seg, i_b, o_b + part[i_b * c:(i_b + 1) * c, :])

    if ctx_out:
        sweep((*ctx_refs, out_refs[0], part_scrs[0]))
    else:
        k_ref, v_ref = ctx_refs
        n = k_ref.shape[0] // c
        for t in range(n):
            f_rows, b_rows = slice(t * c, (t + 1) * c), slice((n - 1 - t) * c, (n - t) * c)
            advance(sf_scr, k_ref[f_rows, :], v_ref[f_rows, :], dec_scr[1], chunk_dec_f)
            advance(sb_scr, k_ref[b_rows, :], v_ref[b_rows, :], dec_scr[3], chunk_dec_b)
    sweep((ql_ref, kl_ref, vl_ref, gl_ref, out_refs[-1], part_scrs[-1]))


def _retention(p_ctx, p_lat, lg_fwd, lg_bwd, batch, ctx_out, casts=()):
    heads = lg_fwd.shape[0]
    n_out = p_lat.shape[1]
    dk = n_out // (6 * heads)
    dv = 2 * dk
    t_c, t_l = p_ctx.shape[0] // batch, p_lat.shape[0] // batch
    c = _tile(math.gcd(t_c, t_l), 256)
    lanes = lambda lg: jnp.broadcast_to(lg.astype(F32)[:, None, None], (heads, 1, V7X_LANES))
    lg_spec = pl.BlockSpec((None, 1, V7X_LANES), lambda b, h: (h, 0, 0))

    def specs(t, full):
        q_blocks = heads if full else 0
        kv = [pl.BlockSpec((t, dk), lambda b, h: (b, q_blocks + h)),
              pl.BlockSpec((t, dv), lambda b, h: (b, (q_blocks + heads) // 2 + h))]
        if not full:
            return kv
        return [pl.BlockSpec((t, dk), lambda b, h: (b, h))] + kv + [
            pl.BlockSpec((t, dv), lambda b, h: (b, 2 * heads + h))]

    mix = lambda p, t: (pl.BlockSpec((t, dv), lambda b, h: (b, h)),
                        jax.ShapeDtypeStruct((p.shape[0], heads * dv), BF16), pltpu.VMEM((t, dv), F32))
    outs = ([mix(p_ctx, t_c)] if ctx_out else []) + [mix(p_lat, t_l)]
    grid = (batch, heads)
    cast_in, cast_out, cast_shapes = _cast_plan(casts, grid)
    out = pl.pallas_call(
        functools.partial(_retention_kernel, chunk=c, reorders=tuple(w[4] for w in casts), ctx_out=ctx_out),
        grid=grid,
        in_specs=[lg_spec, lg_spec] + specs(t_c, ctx_out) + specs(t_l, True) + cast_in,
        out_specs=[o[0] for o in outs] + cast_out,
        out_shape=[o[1] for o in outs] + cast_shapes,
        scratch_shapes=[pltpu.VMEM((dk, dv), F32), pltpu.VMEM((dk, dv), F32), pltpu.VMEM((c, c), F32),
                        pltpu.VMEM((4, c, dk), BF16)] + [o[2] for o in outs],
        compiler_params=_params("arbitrary", "arbitrary"),
        name="retention",
    )(lanes(lg_fwd), lanes(lg_bwd), *[p_ctx] * (4 if ctx_out else 2), *[p_lat] * 4, *[w[0] for w in casts])
    n_mix = len(outs)
    return (out[0] if ctx_out else None), out[n_mix - 1], out[n_mix:]


def _outproj_kernel(*refs, n_parts, seg):
    a_refs, w_refs = refs[:n_parts], refs[n_parts:2 * n_parts]
    res_ref, gate_ref, o_ref = refs[2 * n_parts:]
    for c0 in range(0, o_ref.shape[1], seg):
        cols = slice(c0, c0 + seg)
        acc = functools.reduce(jnp.add, [jnp.dot(a[...], w[:, cols], preferred_element_type=F32)
                                         for a, w in zip(a_refs, w_refs)])
        o_ref[:, cols] = res_ref[:, cols] + gate_ref[:, cols] * acc


def _outproj(parts, weights, res, gate, rows_per_mod):
    n, d = res.shape
    tm = _tile(rows_per_mod, 512)
    return pl.pallas_call(
        functools.partial(_outproj_kernel, n_parts=len(parts), seg=_tile(d, 512, V7X_LANES)),
        grid=(n // tm,),
        in_specs=[_row_spec(tm, a.shape[1]) for a in parts]
        + [pl.BlockSpec((None, w.shape[1], d), lambda i: (0, 0, 0), pipeline_mode=pl.Buffered(1))
           for w in weights]
        + [_row_spec(tm, d), _mod_spec(gate, rows_per_mod, tm)],
        out_specs=_row_spec(tm, d),
        out_shape=jax.ShapeDtypeStruct((n, d), F32),
        compiler_params=_params("parallel"),
        name="outproj",
    )(*parts, *weights, res, gate)


def _ffn_kernel(x_ref, sh_ref, sc_ref, gate_ref, wg_ref, wu_ref, wo_ref, o_ref, u_scr, *, n_steps):
    j = pl.program_id(1)

    @pl.when(j == 0)
    def _():
        _modulate_rows(x_ref, sh_ref, sc_ref, u_scr)
        o_ref[...] = jnp.zeros_like(o_ref)

    u = u_scr[...]
    g = jnp.dot(u, wg_ref[...], preferred_element_type=F32)
    up = jnp.dot(u, wu_ref[...], preferred_element_type=F32)
    act = (_silu(g) * up).astype(BF16)
    o_ref[...] += jnp.dot(act, wo_ref[...], preferred_element_type=F32)

    @pl.when(j == n_steps - 1)
    def _():
        o_ref[...] = x_ref[...] + gate_ref[...] * o_ref[...]


def _ffn(h, shift, scale, gate, rows_per_mod, w_in, w_out):
    n, d = h.shape
    f = w_out.shape[1]
    tm = _tile(rows_per_mod, 1024)
    tf = _tile(f, 512, V7X_LANES)
    nf = f // tf
    n_tiles = n // tm
    return pl.pallas_call(
        functools.partial(_ffn_kernel, n_steps=nf),
        grid=(n_tiles, nf),
        in_specs=[
            _row_spec(tm, d),
            _mod_spec(shift, rows_per_mod, tm), _mod_spec(scale, rows_per_mod, tm),
            _mod_spec(gate, rows_per_mod, tm),
            pl.BlockSpec((None, d, tf), lambda i, j: (0, 0, j)),
            pl.BlockSpec((None, d, tf), lambda i, j: (0, 0, nf + j)),
            pl.BlockSpec((None, tf, d), lambda i, j: (0, j, 0)),
        ],
        out_specs=_row_spec(tm, d),
        out_shape=jax.ShapeDtypeStruct((n, d), F32),
        scratch_shapes=[pltpu.VMEM((tm, d), BF16)],
        compiler_params=_params("parallel", "arbitrary"),
        name="ffn",
    )(h, shift, scale, gate, w_in, w_in, w_out)


def _even_layer(h_ctx, h_lat, mod_ctx, mod_lat, weights, q_gain, k_gain, batch, need_ctx, tables, casts):
    n_ctx = h_ctx.shape[0]
    n_lat = h_lat.shape[0]
    hd = ATTN_HEAD_DIM
    w_in = weights[0]
    q_gain, k_gain = _reorder_gain(q_gain).reshape(1, hd), _reorder_gain(k_gain).reshape(1, hd)
    kvdim = ATTN_KV_HEADS * hd
    fdim = FOURIER_GROUPS * (h_lat.shape[1] // 16)
    dims = (fdim, w_in.shape[2] - fdim - 2 * kvdim, kvdim)

    f_c, q_c, k_c, v_c = _proj_even(h_ctx, mod_ctx[0], mod_ctx[1], n_ctx, w_in, q_gain, k_gain,
                                    *tables["rope_even_ctx"], dims)
    f_l, q_l, k_l, v_l = _proj_even(h_lat, mod_lat[0], mod_lat[1], n_lat // batch, w_in, q_gain, k_gain,
                                    *tables["rope_even_lat"], dims)
    a_l, cast_out = _attention(q_l, [(k_l, v_l), (k_c, v_c)], batch, casts)
    weights = list(weights) + list(cast_out)
    w_f, w_a, ffn_weights, next_weights = weights[1], weights[2], weights[3:5], weights[5:]
    m_l = _fourier(f_l, batch, tables["dft_lat"])
    h_lat = _outproj([m_l, a_l], [w_f, w_a], h_lat, mod_lat[2], n_lat // batch)
    if need_ctx:
        a_c, _ = _attention(q_c, [(k_c, v_c)], batch)
        m_c = _fourier(f_c, batch, tables["dft_ctx"])
        h_ctx = _outproj([m_c, a_c], [w_f, w_a], h_ctx, mod_ctx[2], n_ctx)
    return h_ctx, h_lat, ffn_weights, next_weights


def _odd_layer(h_ctx, h_lat, mod_ctx, mod_lat, weights, lg_fwd, lg_bwd, batch, need_ctx, tables, casts):
    n_ctx = h_ctx.shape[0]
    n_lat = h_lat.shape[0]
    w_in = weights[0]
    qk_dim = w_in.shape[2] // 6
    ctx_columns = None if need_ctx else (qk_dim, 4 * qk_dim)
    p_c = _proj_odd(h_ctx, mod_ctx[0], mod_ctx[1], n_ctx, w_in, tables["rope_odd_ctx"], qk_dim, ctx_columns)
    p_l = _proj_odd(h_lat, mod_lat[0], mod_lat[1], n_lat // batch, w_in, tables["rope_odd_lat"], qk_dim)
    m_c, m_l, cast_out = _retention(p_c, p_l, lg_fwd, lg_bwd, batch, need_ctx, casts)
    weights = list(weights) + list(cast_out)
    w_out, ffn_weights, next_weights = weights[1], weights[2:4], weights[4:]
    h_lat = _outproj([m_l], [w_out], h_lat, mod_lat[2], n_lat // batch)
    if need_ctx:
        h_ctx = _outproj([m_c], [w_out], h_ctx, mod_ctx[2], n_ctx)
    return h_ctx, h_lat, ffn_weights, next_weights


def kernel(x, c, ctx, c_ctx, w_mod, b_mod, w_in_even, w_out_even, q_gain_even, k_gain_even,
           w_in_odd, w_out_odd, log_decay_fwd, log_decay_bwd, w_ffn_in, w_ffn_out):
    batch, seq, d = x.shape
    ctx_len = ctx.shape[1]
    depth = w_mod.shape[0]
    n_lat, n_ctx = batch * seq, batch * ctx_len
    h_lat, h_ctx = x.reshape(n_lat, d), ctx.reshape(n_ctx, d)

    rows = -(-(batch + 1) // 8) * 8
    c_all = jnp.concatenate([c, c_ctx[None], jnp.zeros((rows - batch - 1, d), F32)], axis=0)
    mods = _modulation(c_all, w_mod, b_mod).reshape(depth, rows, N_MOD, d)

    fdim = FOURIER_GROUPS * (d // 16)

    def weight_slices(i):
        j = i // 2
        if i % 2 == 0:
            qk_hi = w_in_even.shape[2] - ATTN_KV_HEADS * ATTN_HEAD_DIM
            mixer = [(w_in_even, j, 0, d, (ATTN_HEAD_DIM, fdim, qk_hi)),
                     (w_out_even, j, 0, fdim, None), (w_out_even, j, fdim, w_out_even.shape[1], None)]
        else:
            mixer = [(w_in_odd, j, 0, d, (ret_hd, 0, w_in_odd.shape[2] // 3)),
                     (w_out_odd, j, 0, w_out_odd.shape[1], None)]
        return mixer + [(w_ffn_in, i, 0, d, None), (w_ffn_out, i, 0, w_ffn_out.shape[1], None)]

    ret_hd = w_in_odd.shape[2] // (6 * log_decay_fwd.shape[1])
    ctx_rows = _tile(n_ctx, 1024)
    tables = {
        "rope_even_lat": _rope_tables(seq, ATTN_HEAD_DIM, False),
        "rope_even_ctx": _rope_tables(ctx_rows, ATTN_HEAD_DIM, True),
        "rope_odd_lat": _proj_odd_tables(*_rope_tables(seq, ret_hd, False), ret_hd ** -0.5),
        "rope_odd_ctx": _proj_odd_tables(*_rope_tables(ctx_rows, ret_hd, True), ret_hd ** -0.5),
        "dft_lat": _dft_tables(seq, d // 16),
        "dft_ctx": _dft_tables(ctx_len, d // 16),
    }

    first = weight_slices(0)
    weights, late = [_to_bf16(*first[0])], first[1:]
    for i in range(depth):
        need_ctx = i < depth - 1
        casts = late + (weight_slices(i + 1) if i + 1 < depth else [])
        mod_lat = [mods[i, :batch, k][:, None, :] for k in range(N_MOD)]
        mod_ctx = [mods[i, batch:batch + 1, k][:, None, :] for k in range(N_MOD)]
        j = i // 2
        if i % 2 == 0:
            h_ctx, h_lat, ffn_weights, weights = _even_layer(
                h_ctx, h_lat, mod_ctx, mod_lat, weights, q_gain_even[j], k_gain_even[j], batch, need_ctx,
                tables, casts)
        else:
            h_ctx, h_lat, ffn_weights, weights = _odd_layer(
                h_ctx, h_lat, mod_ctx, mod_lat, weights, log_decay_fwd[j], log_decay_bwd[j], batch, need_ctx,
                tables, casts)
        late = []
        h_lat = _ffn(h_lat, mod_lat[3], mod_lat[4], mod_lat[5], n_lat // batch, *ffn_weights)
        if need_ctx:
            h_ctx = _ffn(h_ctx, mod_ctx[3], mod_ctx[4], mod_ctx[5], n_ctx, *ffn_weights)
    return h_lat.reshape(batch, seq, d)
```
